```python
import math
import jax, jax.numpy as jnp
from jax import lax
import numpy as np

D_MODEL = 1024
BATCH = 8
SEQ = 4096
DEPTH = 1

SSM_WIDTH = D_MODEL // 2
SSM_GROUP = 16
SSM_GROUPS = SSM_WIDTH // SSM_GROUP
SSM_STATE = 64
DT_MIN = 1e-3
DT_MAX = 1e-1
MLA_HEADS = 8
MLA_NOPE = 64
MLA_ROPE = 32
MLA_V = 64
MLA_Q_LORA = 384
MLA_KV_LORA = 256
ROPE_THETA = 10000.0
Q_BLOCK = 128
N_BRANCHES = 2
IN_SPLITS = [SSM_WIDTH,
             SSM_WIDTH + MLA_Q_LORA,
             SSM_WIDTH + MLA_Q_LORA + MLA_KV_LORA,
             SSM_WIDTH + MLA_Q_LORA + MLA_KV_LORA + MLA_ROPE]
IN_WIDTH = SSM_WIDTH + MLA_Q_LORA + MLA_KV_LORA + MLA_ROPE + N_BRANCHES * D_MODEL
MEM_TOKENS = 256
MEM_HEADS = 4
MEM_HEAD_DIM = D_MODEL // MEM_HEADS
PEER_HEADS = 8
PEER_N_KEYS = 128
PEER_EXPERTS = PEER_N_KEYS * PEER_N_KEYS
PEER_KEY_DIM = 256
PEER_HALF = PEER_KEY_DIM // 2
PEER_TOPK = 16
PEER_TOKEN_BLOCK = 128
NORM_EPS = 1e-6

kernel_name = "hybrid_s5_mla_peer_encoder"


def rms_norm(x, g):
    xf = x.astype(jnp.float32)
    y = xf * lax.rsqrt(jnp.mean(xf * xf, axis=-1, keepdims=True) + NORM_EPS)
    return (y * g.astype(jnp.float32)).astype(x.dtype)


def rope_tables(positions):
    inv_freq = 1.0 / (ROPE_THETA ** (jnp.arange(0, MLA_ROPE, 2, dtype=jnp.float32) / MLA_ROPE))
    ang = positions.astype(jnp.float32)[..., None] * inv_freq
    return jnp.cos(ang), jnp.sin(ang)


def apply_rope(t, cos, sin):
    t1, t2 = jnp.split(t, 2, axis=-1)
    cos = cos.astype(t.dtype)
    sin = sin.astype(t.dtype)
    return jnp.concatenate([t1 * cos - t2 * sin, t2 * cos + t1 * sin], axis=-1)


def _complex_linear_combine(left, right):
    ar1, ai1, br1, bi1 = left
    ar2, ai2, br2, bi2 = right
    ar = ar2 * ar1 - ai2 * ai1
    ai = ar2 * ai1 + ai2 * ar1
    br = ar2 * br1 - ai2 * bi1 + br2
    bi = ar2 * bi1 + ai2 * br1 + bi2
    return (ar, ai, br, bi)


def s5_direction(u, lam_re, lam_im, log_dt, b_re, b_im, c_re, c_im, reverse):
    f32 = jnp.float32
    lam_re = lam_re.astype(f32); lam_im = lam_im.astype(f32)
    dt = jnp.exp(log_dt.astype(f32))[:, None]
    mag = jnp.exp(lam_re * dt)
    ar = mag * jnp.cos(lam_im * dt)
    ai = mag * jnp.sin(lam_im * dt)
    nr = ar - 1.0
    den = lam_re * lam_re + lam_im * lam_im
    fr = (nr * lam_re + ai * lam_im) / den
    fi = (ai * lam_re - nr * lam_im) / den
    b_re = b_re.astype(f32); b_im = b_im.astype(f32)
    bb_re = fr[..., None] * b_re - fi[..., None] * b_im
    bb_im = fr[..., None] * b_im + fi[..., None] * b_re
    bu_re = jnp.einsum('blgc,gpc->blgp', u, bb_re)
    bu_im = jnp.einsum('blgc,gpc->blgp', u, bb_im)
    L = u.shape[1]
    a_re = jnp.broadcast_to(ar[None, None], (1, L) + ar.shape)
    a_im = jnp.broadcast_to(ai[None, None], (1, L) + ai.shape)
    _, _, s_re, s_im = lax.associative_scan(
        _complex_linear_combine, (a_re, a_im, bu_re, bu_im), reverse=reverse, axis=1)
    return (jnp.einsum('blgp,gcp->blgc', s_re, c_re.astype(f32))
            - jnp.einsum('blgp,gcp->blgc', s_im, c_im.astype(f32)))


def mla_attention(q_nope, q_rope, k_nope, k_rope, v):
    Bsz, L, H, _ = q_nope.shape
    nb = L // Q_BLOCK
    scale = (MLA_NOPE + MLA_ROPE) ** -0.5

    def to_blocks(t):
        return jnp.moveaxis(t.reshape((Bsz, nb, Q_BLOCK) + t.shape[2:]), 1, 0)

    def block(qs):
        qn, qr = qs
        s = (jnp.einsum('bqhd,bkhd->bhqk', qn, k_nope, preferred_element_type=jnp.float32)
             + jnp.einsum('bqhr,bkr->bhqk', qr, k_rope, preferred_element_type=jnp.float32)) * scale
        p = jax.nn.softmax(s, axis=-1).astype(v.dtype)
        return jnp.einsum('bhqk,bkhd->bqhd', p, v)

    o = lax.map(block, (to_blocks(q_nope), to_blocks(q_rope)))
    return jnp.moveaxis(o, 0, 1).reshape(Bsz, L, H * MLA_V)


def mixer_block(xn, cos, sin, w_in, b_gate, lam_re, lam_im, log_dt, b_re, b_im, c_re, c_im,
                d_skip, w_ssm_glu, g_q_lora, w_uq, g_kv_lora, w_ukv, w_mla_o, w_out):
    Bsz, L, _ = xn.shape
    proj = xn @ w_in
    u_ssm, c_q, c_kv, k_rope, gate_logits = jnp.split(proj, IN_SPLITS, axis=-1)

    u = u_ssm.reshape(Bsz, L, SSM_GROUPS, SSM_GROUP).astype(jnp.float32)
    y = (s5_direction(u, lam_re[0], lam_im[0], log_dt[0], b_re[0], b_im[0], c_re[0], c_im[0], False)
         + s5_direction(u, lam_re[1], lam_im[1], log_dt[1], b_re[1], b_im[1], c_re[1], c_im[1], True))
    y = y.reshape(Bsz, L, SSM_WIDTH) + d_skip.astype(jnp.float32) * u_ssm.astype(jnp.float32)
    y = jax.nn.gelu(y).astype(xn.dtype)
    glu_val, glu_gate = jnp.split(y @ w_ssm_glu, 2, axis=-1)
    branch_ssm = glu_val * jax.nn.sigmoid(glu_gate)

    c_q = rms_norm(c_q, g_q_lora)
    q = (c_q @ w_uq).reshape(Bsz, L, MLA_HEADS, MLA_NOPE + MLA_ROPE)
    q_nope = q[..., :MLA_NOPE]
    q_rope = apply_rope(q[..., MLA_NOPE:], cos[:, :, None], sin[:, :, None])
    c_kv = rms_norm(c_kv, g_kv_lora)
    kv = (c_kv @ w_ukv).reshape(Bsz, L, MLA_HEADS, MLA_NOPE + MLA_V)
    k_nope = kv[..., :MLA_NOPE]
    v = kv[..., MLA_NOPE:]
    k_rope = apply_rope(k_rope, cos, sin)
    branch_mla = mla_attention(q_nope, q_rope, k_nope, k_rope, v) @ w_mla_o

    gates = jax.nn.sigmoid((gate_logits + b_gate).astype(jnp.float32)).astype(xn.dtype)
    g_ssm, g_mla = jnp.split(gates, N_BRANCHES, axis=-1)
    return (g_ssm * branch_ssm + g_mla * branch_mla) @ w_out


def memory_cross_attention(hn, memn, w_q, w_kv, w_o):
    Bsz, L, _ = hn.shape
    M = memn.shape[1]
    q = (hn @ w_q).reshape(Bsz, L, MEM_HEADS, MEM_HEAD_DIM)
    k, v = jnp.split(memn @ w_kv, 2, axis=-1)
    k = k.reshape(Bsz, M, MEM_HEADS, MEM_HEAD_DIM)
    v = v.reshape(Bsz, M, MEM_HEADS, MEM_HEAD_DIM)
    s = jnp.einsum('blhd,bmhd->bhlm', q, k, preferred_element_type=jnp.float32) * (MEM_HEAD_DIM ** -0.5)
    p = jax.nn.softmax(s, axis=-1).astype(v.dtype)
    o = jnp.einsum('bhlm,bmhd->blhd', p, v).reshape(Bsz, L, D_MODEL)
    return o @ w_o


def peer_ffn(hn, w_q, sub_keys, u_emb, v_emb):
    Bsz, L, D = hn.shape
    tokens = hn.reshape(-1, PEER_TOKEN_BLOCK, D)

    def block(xc):
        T = xc.shape[0]
        q = (xc @ w_q).reshape(T, PEER_HEADS, 2, PEER_HALF)
        s = jnp.einsum('thsd,hsnd->thsn', q, sub_keys, preferred_element_type=jnp.float32)
        s1, i1 = lax.top_k(s[:, :, 0], PEER_TOPK)
        s2, i2 = lax.top_k(s[:, :, 1], PEER_TOPK)
        cand_s = (s1[..., :, None] + s2[..., None, :]).reshape(T, PEER_HEADS, PEER_TOPK * PEER_TOPK)
        cand_i = (i1[..., :, None] * PEER_N_KEYS + i2[..., None, :]).reshape(T, PEER_HEADS, PEER_TOPK * PEER_TOPK)
        top_s, pos = lax.top_k(cand_s, PEER_TOPK)
        idx = jnp.take_along_axis(cand_i, pos, axis=-1)
        gate = jax.nn.softmax(top_s, axis=-1)
        u = jnp.take(u_emb, idx, axis=0)
        act = jax.nn.gelu(jnp.einsum('thkd,td->thk', u, xc).astype(jnp.float32))
        vv = jnp.take(v_emb, idx, axis=0)
        return jnp.einsum('thk,thkd->td', (gate * act).astype(vv.dtype), vv)

    return lax.map(block, tokens).reshape(Bsz, L, D)


def setup_inputs(seed: int = 0) -> dict:
    key = jax.random.key(seed)
    ks = iter(jax.random.split(key, 48))

    def nrm(shape, std):
        return std * jax.random.normal(next(ks), shape, jnp.float32)

    def gain(shape):
        return 1.0 + nrm(shape, 0.02)

    G, P, Cg = SSM_GROUPS, SSM_STATE, SSM_GROUP
    x = nrm((BATCH, SEQ, D_MODEL), 1.0)
    mem = nrm((BATCH, MEM_TOKENS, D_MODEL), 1.0)
    positions = jnp.broadcast_to(jnp.arange(SEQ, dtype=jnp.int32)[None, :], (BATCH, SEQ))
    g_mix = gain((DEPTH, D_MODEL))
    w_in = nrm((DEPTH, D_MODEL, IN_WIDTH), D_MODEL ** -0.5)
    b_gate = nrm((DEPTH, N_BRANCHES * D_MODEL), 0.02)
    ssm_lam_re = -0.5 + nrm((DEPTH, 2, G, P), 0.01)
    ssm_lam_im = math.pi * jnp.arange(P, dtype=jnp.float32) + nrm((DEPTH, 2, G, P), 0.01)
    ssm_log_dt = jax.random.uniform(next(ks), (DEPTH, 2, G), jnp.float32,
                                    math.log(DT_MIN), math.log(DT_MAX))
    ssm_b_re = nrm((DEPTH, 2, G, P, Cg), (2 * Cg) ** -0.5)
    ssm_b_im = nrm((DEPTH, 2, G, P, Cg), (2 * Cg) ** -0.5)
    ssm_c_re = nrm((DEPTH, 2, G, Cg, P), P ** -0.5)
    ssm_c_im = nrm((DEPTH, 2, G, Cg, P), P ** -0.5)
    ssm_d = nrm((DEPTH, SSM_WIDTH), 1.0)
    w_ssm_glu = nrm((DEPTH, SSM_WIDTH, 2 * D_MODEL), SSM_WIDTH ** -0.5)
    g_q_lora = gain((DEPTH, MLA_Q_LORA))
    w_uq = nrm((DEPTH, MLA_Q_LORA, MLA_HEADS * (MLA_NOPE + MLA_ROPE)), MLA_Q_LORA ** -0.5)
    g_kv_lora = gain((DEPTH, MLA_KV_LORA))
    w_ukv = nrm((DEPTH, MLA_KV_LORA, MLA_HEADS * (MLA_NOPE + MLA_V)), MLA_KV_LORA ** -0.5)
    w_mla_o = nrm((DEPTH, MLA_HEADS * MLA_V, D_MODEL), (MLA_HEADS * MLA_V) ** -0.5)
    w_out = nrm((DEPTH, D_MODEL, D_MODEL), D_MODEL ** -0.5)
    g_mem_q = gain((DEPTH, D_MODEL))
    g_mem_kv = gain((DEPTH, D_MODEL))
    w_mem_q = nrm((DEPTH, D_MODEL, D_MODEL), D_MODEL ** -0.5)
    w_mem_kv = nrm((DEPTH, D_MODEL, 2 * D_MODEL), D_MODEL ** -0.5)
    w_mem_o = nrm((DEPTH, D_MODEL, D_MODEL), D_MODEL ** -0.5)
    g_peer = gain((DEPTH, D_MODEL))
    w_peer_q = nrm((DEPTH, D_MODEL, PEER_HEADS * PEER_KEY_DIM), D_MODEL ** -0.5)
    peer_sub_keys = nrm((DEPTH, PEER_HEADS, 2, PEER_N_KEYS, PEER_HALF), PEER_HALF ** -0.5)
    peer_u = nrm((DEPTH, PEER_EXPERTS, D_MODEL), D_MODEL ** -0.5)
    peer_v = nrm((DEPTH, PEER_EXPERTS, D_MODEL), PEER_HEADS ** -0.5)
    g_final = gain((D_MODEL,))
    return {"x": x, "mem": mem, "positions": positions, "g_mix": g_mix, "w_in": w_in,
            "b_gate": b_gate, "ssm_lam_re": ssm_lam_re, "ssm_lam_im": ssm_lam_im,
            "ssm_log_dt": ssm_log_dt, "ssm_b_re": ssm_b_re, "ssm_b_im": ssm_b_im,
            "ssm_c_re": ssm_c_re, "ssm_c_im": ssm_c_im, "ssm_d": ssm_d, "w_ssm_glu": w_ssm_glu,
            "g_q_lora": g_q_lora, "w_uq": w_uq, "g_kv_lora": g_kv_lora, "w_ukv": w_ukv,
            "w_mla_o": w_mla_o, "w_out": w_out, "g_mem_q": g_mem_q, "g_mem_kv": g_mem_kv,
            "w_mem_q": w_mem_q, "w_mem_kv": w_mem_kv, "w_mem_o": w_mem_o, "g_peer": g_peer,
            "w_peer_q": w_peer_q, "peer_sub_keys": peer_sub_keys, "peer_u": peer_u,
            "peer_v": peer_v, "g_final": g_final}


def reference(x, mem, positions, g_mix, w_in, b_gate, ssm_lam_re, ssm_lam_im, ssm_log_dt,
              ssm_b_re, ssm_b_im, ssm_c_re, ssm_c_im, ssm_d, w_ssm_glu, g_q_lora, w_uq,
              g_kv_lora, w_ukv, w_mla_o, w_out, g_mem_q, g_mem_kv, w_mem_q, w_mem_kv, w_mem_o,
              g_peer, w_peer_q, peer_sub_keys, peer_u, peer_v, g_final):
    cos, sin = rope_tables(positions)
    h = x
    for l in range(DEPTH):
        h = h + mixer_block(rms_norm(h, g_mix[l]), cos, sin, w_in[l], b_gate[l],
                            ssm_lam_re[l], ssm_lam_im[l], ssm_log_dt[l], ssm_b_re[l], ssm_b_im[l],
                            ssm_c_re[l], ssm_c_im[l], ssm_d[l], w_ssm_glu[l], g_q_lora[l], w_uq[l],
                            g_kv_lora[l], w_ukv[l], w_mla_o[l], w_out[l])
        h = h + memory_cross_attention(rms_norm(h, g_mem_q[l]), rms_norm(mem, g_mem_kv[l]),
                                       w_mem_q[l], w_mem_kv[l], w_mem_o[l])
        h = h + peer_ffn(rms_norm(h, g_peer[l]), w_peer_q[l], peer_sub_keys[l], peer_u[l], peer_v[l])
    return rms_norm(h, g_final)
```

```python
import functools
import math

import jax
import jax.numpy as jnp
from jax import lax
from jax.experimental import pallas as pl
from jax.experimental.pallas import tpu as pltpu

F32 = jnp.float32
BF16 = jnp.bfloat16

D_MODEL = 1024
SSM_WIDTH = 512
SSM_GROUP = 16
SSM_GROUPS = 32
SSM_STATE = 64
MLA_HEADS = 8
MLA_NOPE = 64
MLA_ROPE = 32
MLA_V = 64
MLA_Q_LORA = 384
MLA_KV_LORA = 256
ROPE_THETA = 10000.0
MEM_HEADS = 4
MEM_HEAD_DIM = 256
PEER_HEADS = 8
PEER_N_KEYS = 128
PEER_HALF = 128
PEER_TOPK = 16
NORM_EPS = 1e-6

LANES = 128
SUBLANES = 8
VMEM_LIMIT = 48 * 1024 * 1024

HEAD_PAD = LANES
SSM_CHUNK = LANES
SSM_CHUNKS = SSM_WIDTH // SSM_CHUNK
SSM_CHUNK_STATES = (SSM_CHUNK // SSM_GROUP) * SSM_STATE
SSM_STATES = SSM_GROUPS * SSM_STATE


def _cparams(sem):
    return pltpu.CompilerParams(dimension_semantics=sem, vmem_limit_bytes=VMEM_LIMIT)


def _rms(x, g):
    return x * lax.rsqrt(jnp.mean(x * x, axis=-1, keepdims=True) + NORM_EPS) * g


def _bdot(a, b):
    return jnp.dot(a, b, preferred_element_type=F32)


def _gelu(x):
    return jax.nn.gelu(x, approximate=True)


def _full(shape):
    nd = len(shape)
    return pl.BlockSpec(shape, lambda *_: (0,) * nd)


def _in_proj_kernel(x_ref, pos_ref, gmix_ref, wa_ref, wg_ref, bg_ref, gq_ref, wq_ref,
                    gkv_ref, wkv_ref, invf_ref, cmask_ref, ssign_ref, vone_ref,
                    u_ref, q_ref, k_ref, v_ref, gate_ref):
    xn = _rms(x_ref[...], gmix_ref[...]).astype(BF16)
    proj = _bdot(xn, wa_ref[...])
    u_ref[...] = proj[:, :SSM_WIDTH]
    gate_ref[...] = jax.nn.sigmoid(_bdot(xn, wg_ref[...]) + bg_ref[...]).astype(BF16)

    ang = pos_ref[...].astype(F32) * invf_ref[...]
    cosf = jnp.cos(ang) * cmask_ref[...]
    sinf = jnp.sin(ang) * ssign_ref[...]

    o = SSM_WIDTH
    cq = _rms(proj[:, o:o + MLA_Q_LORA], gq_ref[...]).astype(BF16)
    o += MLA_Q_LORA
    ckv = _rms(proj[:, o:o + MLA_KV_LORA], gkv_ref[...]).astype(BF16)
    o += MLA_KV_LORA
    kr = proj[:, o:o + HEAD_PAD]
    krs = proj[:, o + HEAD_PAD:o + 2 * HEAD_PAD]
    kr_rot = kr * cosf + krs * sinf

    qq = _bdot(cq, wq_ref[...])
    kv = _bdot(ckv, wkv_ref[...])
    scale = (MLA_NOPE + MLA_ROPE) ** -0.5
    hw = MLA_HEADS * HEAD_PAD
    for h in range(MLA_HEADS):
        sl = slice(h * HEAD_PAD, (h + 1) * HEAD_PAD)
        sl2 = slice(hw + h * HEAD_PAD, hw + (h + 1) * HEAD_PAD)
        q_ref[:, sl] = ((qq[:, sl] * cosf + qq[:, sl2] * sinf) * scale).astype(BF16)
        k_ref[:, sl] = (kv[:, sl] + kr_rot).astype(BF16)
    v_ref[...] = (kv[:, hw:] + vone_ref[...]).astype(BF16)


def _in_proj(x2, pos2, gmix, wa, wg, bg, gq, wq2, gkv, wkv2, invf, cmask, ssign, vone, B, L, tl):
    N = B * L
    nl = L // tl
    hw = MLA_HEADS * HEAD_PAD
    row = lambda b, i: (b * nl + i, 0)
    consts = [gmix, wa, wg, bg, gq, wq2, gkv, wkv2, invf, cmask, ssign, vone]
    return pl.pallas_call(
        _in_proj_kernel,
        grid=(B, nl),
        in_specs=[pl.BlockSpec((tl, D_MODEL), row), pl.BlockSpec((tl, 1), row)]
        + [_full(c.shape) for c in consts],
        out_specs=[pl.BlockSpec((tl, SSM_WIDTH), lambda b, i: (i, b)),
                   pl.BlockSpec((tl, hw), row), pl.BlockSpec((tl, hw), row),
                   pl.BlockSpec((tl, hw), row), pl.BlockSpec((tl, 2 * D_MODEL), row)],
        out_shape=[jax.ShapeDtypeStruct((L, B * SSM_WIDTH), F32),
                   jax.ShapeDtypeStruct((N, hw), BF16), jax.ShapeDtypeStruct((N, hw), BF16),
                   jax.ShapeDtypeStruct((N, hw), BF16), jax.ShapeDtypeStruct((N, 2 * D_MODEL), BF16)],
        compiler_params=_cparams(("arbitrary", "arbitrary")),
        name="in_proj",
    )(x2, pos2, *consts)


def _s5_disc_kernel(lr_ref, li_ref, ldt_ref, bre_ref, bim_ref, ar_ref, ai_ref, bbr_ref, bbi_ref):
    lr = lr_ref[...]
    li = li_ref[...]
    dt = jnp.exp(ldt_ref[...])
    mag = jnp.exp(lr * dt)
    ar = mag * jnp.cos(li * dt)
    ai = mag * jnp.sin(li * dt)
    nr = ar - 1.0
    den = lr * lr + li * li
    fr = (nr * lr + ai * li) / den
    fi = (ai * lr - nr * li) / den
    ar_ref[...] = ar
    ai_ref[...] = ai
    bre = bre_ref[...]
    bim = bim_ref[...]
    bbr_ref[...] = fr * bre - fi * bim
    bbi_ref[...] = fr * bim + fi * bre


def _s5_discretise(lam_re, lam_im, log_dt, b_re, b_im):
    n = lam_re.size
    ldt = jnp.broadcast_to(log_dt[..., None], lam_re.shape).reshape(n, 1)
    sds = jax.ShapeDtypeStruct
    ar, ai, bbr, bbi = pl.pallas_call(
        _s5_disc_kernel,
        out_shape=[sds((n, 1), F32), sds((n, 1), F32), sds((n, SSM_GROUP), F32), sds((n, SSM_GROUP), F32)],
        name="s5_discretise",
    )(lam_re.reshape(n, 1), lam_im.reshape(n, 1), ldt,
      b_re.reshape(n, SSM_GROUP), b_im.reshape(n, SSM_GROUP))
    return (ar.reshape(lam_re.shape), ai.reshape(lam_re.shape),
            bbr.reshape(b_re.shape), bbi.reshape(b_re.shape))


def _s5_kernel(uf_ref, ub_ref, bm_ref, cm_ref, ar_ref, ai_ref, yf_ref, yb_ref,
               st_ref, sf_ref, sb_ref, *, tt, nb):
    S = SSM_STATES
    CS = SSM_CHUNK_STATES

    @pl.when(pl.program_id(0) == 0)
    def _():
        st_ref[...] = jnp.zeros_like(st_ref)

    for d, (u_ref, s_ref) in enumerate(((uf_ref, sf_ref), (ub_ref, sb_ref))):
        ub = u_ref[...].astype(BF16)
        for j in range(SSM_CHUNKS):
            bu = _bdot(ub[:, j * SSM_CHUNK:(j + 1) * SSM_CHUNK], bm_ref[d, j])
            s_ref[:, j * CS:(j + 1) * CS] = bu[:, :CS]
            s_ref[:, S + j * CS:S + (j + 1) * CS] = bu[:, CS:]

    for c in range(S // CS):
        lre = slice(c * CS, (c + 1) * CS)
        lim = slice(S + c * CS, S + (c + 1) * CS)
        arf = jnp.broadcast_to(ar_ref[0, :, lre], (nb, CS))
        aif = jnp.broadcast_to(ai_ref[0, :, lre], (nb, CS))
        arb = jnp.broadcast_to(ar_ref[1, :, lre], (nb, CS))
        aib = jnp.broadcast_to(ai_ref[1, :, lre], (nb, CS))

        def body(t, carry):
            fr, fi, br, bi = carry
            rf = pl.multiple_of(t * nb, nb)
            rb = pl.multiple_of((tt - 1 - t) * nb, nb)
            nfr = arf * fr - aif * fi + sf_ref[pl.ds(rf, nb), lre]
            nfi = arf * fi + aif * fr + sf_ref[pl.ds(rf, nb), lim]
            sf_ref[pl.ds(rf, nb), lre] = nfr
            sf_ref[pl.ds(rf, nb), lim] = nfi
            nbr = arb * br - aib * bi + sb_ref[pl.ds(rb, nb), lre]
            nbi = arb * bi + aib * br + sb_ref[pl.ds(rb, nb), lim]
            sb_ref[pl.ds(rb, nb), lre] = nbr
            sb_ref[pl.ds(rb, nb), lim] = nbi
            return nfr, nfi, nbr, nbi

        init = (st_ref[0, :, lre], st_ref[0, :, lim], st_ref[1, :, lre], st_ref[1, :, lim])
        fr, fi, br, bi = lax.fori_loop(0, tt, body, init)
        st_ref[0, :, lre] = fr
        st_ref[0, :, lim] = fi
        st_ref[1, :, lre] = br
        st_ref[1, :, lim] = bi

    for d, (s_ref, y_ref) in enumerate(((sf_ref, yf_ref), (sb_ref, yb_ref))):
        for j in range(SSM_CHUNKS):
            sre = s_ref[:, j * CS:(j + 1) * CS].astype(BF16)
            sim = s_ref[:, S + j * CS:S + (j + 1) * CS].astype(BF16)
            y_ref[:, j * SSM_CHUNK:(j + 1) * SSM_CHUNK] = (
                _bdot(sre, cm_ref[d, j, :CS, :]) + _bdot(sim, cm_ref[d, j, CS:, :]))


def _s5(u_tm, bm, cm, ar, ai, B, L, tt):
    nblk = L // tt
    R = tt * B
    blk = lambda idx: pl.BlockSpec((R, SSM_WIDTH), idx)
    fwd = lambda i: (i, 0)
    bwd = lambda i: (nblk - 1 - i, 0)
    return pl.pallas_call(
        functools.partial(_s5_kernel, tt=tt, nb=B),
        grid=(nblk,),
        in_specs=[blk(fwd), blk(bwd), _full(bm.shape), _full(cm.shape), _full(ar.shape), _full(ai.shape)],
        out_specs=[blk(fwd), blk(bwd)],
        out_shape=[jax.ShapeDtypeStruct((L * B, SSM_WIDTH), F32)] * 2,
        scratch_shapes=[pltpu.VMEM((2, B, 2 * SSM_STATES), F32),
                        pltpu.VMEM((R, 2 * SSM_STATES), F32),
                        pltpu.VMEM((R, 2 * SSM_STATES), F32)],
        compiler_params=_cparams(("arbitrary",)),
        name="s5_scan",
    )(u_tm, u_tm, bm, cm, ar, ai)


def _attn_kernel(q_ref, k_ref, v_ref, o_ref):
    s = lax.dot_general(q_ref[...], k_ref[...], (((1,), (1,)), ((), ())),
                        preferred_element_type=F32)
    m = jnp.max(s, axis=-1, keepdims=True)
    p = jnp.exp(s - m).astype(BF16)
    o = _bdot(p, v_ref[...])
    o_ref[...] = (o / o[:, MLA_V:MLA_V + 1]).astype(BF16)


def _attention(q, k, v, B, L, tq):
    nq = L // tq
    return pl.pallas_call(
        _attn_kernel,
        grid=(B, MLA_HEADS, nq),
        in_specs=[pl.BlockSpec((tq, HEAD_PAD), lambda b, h, i: (b * nq + i, h)),
                  pl.BlockSpec((L, HEAD_PAD), lambda b, h, i: (b, h)),
                  pl.BlockSpec((L, HEAD_PAD), lambda b, h, i: (b, h))],
        out_specs=pl.BlockSpec((tq, HEAD_PAD), lambda b, h, i: (b * nq + i, h)),
        out_shape=jax.ShapeDtypeStruct(q.shape, BF16),
        compiler_params=_cparams(("arbitrary", "arbitrary", "arbitrary")),
        name="mla_attention",
    )(q, k, v)


def _merge_kernel(x_ref, yf_ref, yb_ref, u_ref, at_ref, gate_ref, d_ref, wglu_ref, wmo_ref, wout_ref,
                  h_ref):
    y = yf_ref[...] + yb_ref[...] + d_ref[...] * u_ref[...]
    y = _gelu(y).astype(BF16)
    glu = _bdot(y, wglu_ref[...])
    branch_ssm = glu[:, :D_MODEL] * jax.nn.sigmoid(glu[:, D_MODEL:])
    branch_mla = _bdot(at_ref[...], wmo_ref[...])
    g = gate_ref[...].astype(F32)
    merged = (g[:, :D_MODEL] * branch_ssm + g[:, D_MODEL:] * branch_mla).astype(BF16)
    h_ref[...] = x_ref[...] + _bdot(merged, wout_ref[...])


def _merge(x2, yf_tm, yb_tm, u_tm, attn, gates, d_skip, wglu, wmo, wout, B, L, tl):
    nl = L // tl
    row = lambda b, i: (b * nl + i, 0)
    tm = pl.BlockSpec((tl, SSM_WIDTH), lambda b, i: (i, b))
    consts = [d_skip, wglu, wmo, wout]
    return pl.pallas_call(
        _merge_kernel,
        grid=(B, nl),
        in_specs=[pl.BlockSpec((tl, D_MODEL), row), tm, tm, tm,
                  pl.BlockSpec((tl, attn.shape[1]), row), pl.BlockSpec((tl, 2 * D_MODEL), row)]
        + [_full(c.shape) for c in consts],
        out_specs=pl.BlockSpec((tl, D_MODEL), row),
        out_shape=jax.ShapeDtypeStruct(x2.shape, F32),
        compiler_params=_cparams(("arbitrary", "arbitrary")),
        name="merge_out",
    )(x2, yf_tm, yb_tm, u_tm, attn, gates, *consts)


def _mem_kv_kernel(mem_ref, g_ref, w_ref, kv_ref):
    mn = _rms(mem_ref[...], g_ref[...]).astype(BF16)
    kv_ref[...] = _bdot(mn, w_ref[...]).astype(BF16)


def _mem_kv(mem2, g, w, tm):
    n = mem2.shape[0]
    return pl.pallas_call(
        _mem_kv_kernel,
        grid=(n // tm,),
        in_specs=[pl.BlockSpec((tm, D_MODEL), lambda i: (i, 0)), _full(g.shape), _full(w.shape)],
        out_specs=pl.BlockSpec((tm, 2 * D_MODEL), lambda i: (i, 0)),
        out_shape=jax.ShapeDtypeStruct((n, 2 * D_MODEL), BF16),
        compiler_params=_cparams(("arbitrary",)),
        name="mem_kv",
    )(mem2, g, w)


def _mem_attn_kernel(h_ref, kv_ref, g_ref, wq_ref, wo_ref, o_ref):
    h = h_ref[...]
    q = _bdot(_rms(h, g_ref[...]).astype(BF16), wq_ref[...])
    scale = MEM_HEAD_DIM ** -0.5
    outs = []
    for hd in range(MEM_HEADS):
        sl = slice(hd * MEM_HEAD_DIM, (hd + 1) * MEM_HEAD_DIM)
        kh = kv_ref[:, sl]
        vh = kv_ref[:, D_MODEL + hd * MEM_HEAD_DIM:D_MODEL + (hd + 1) * MEM_HEAD_DIM]
        s = lax.dot_general(q[:, sl].astype(BF16), kh, (((1,), (1,)), ((), ())),
                            preferred_element_type=F32) * scale
        p = jnp.exp(s - jnp.max(s, axis=-1, keepdims=True))
        p = p / jnp.sum(p, axis=-1, keepdims=True)
        outs.append(_bdot(p.astype(BF16), vh).astype(BF16))
    o = jnp.concatenate(outs, axis=-1)
    o_ref[...] = h + _bdot(o, wo_ref[...])


def _mem_attn(h2, kv, g, wq, wo, B, L, M, tl):
    nl = L // tl
    row = lambda b, i: (b * nl + i, 0)
    consts = [g, wq, wo]
    return pl.pallas_call(
        _mem_attn_kernel,
        grid=(B, nl),
        in_specs=[pl.BlockSpec((tl, D_MODEL), row), pl.BlockSpec((M, 2 * D_MODEL), lambda b, i: (b, 0))]
        + [_full(c.shape) for c in consts],
        out_specs=pl.BlockSpec((tl, D_MODEL), row),
        out_shape=jax.ShapeDtypeStruct(h2.shape, F32),
        compiler_params=_cparams(("arbitrary", "arbitrary")),
        name="mem_attention",
    )(h2, kv, *consts)


_NEG_INF = float("-inf")
_BIG_I32 = 2 ** 30


def _topk_rows(s, payload, k):
    vals, pays = [], []
    for _ in range(k):
        m = jnp.max(s, axis=0, keepdims=True)
        p = jnp.min(jnp.where(s == m, payload, _BIG_I32), axis=0, keepdims=True)
        vals.append(m)
        pays.append(p)
        s = jnp.where(payload == p, _NEG_INF, s)
    return jnp.concatenate(vals, axis=0), jnp.concatenate(pays, axis=0)


def _peer_candidates(s1, i1, s2, i2, T):
    K = PEER_TOPK
    r8 = lax.broadcasted_iota(jnp.int32, (8, T), 0)
    r16 = lax.broadcasted_iota(jnp.int32, (K, T), 0)
    cs = [s1[0:1] + s2]
    ci = [i1[0:1] * PEER_N_KEYS + i2]
    pos = [r16]
    for r1 in range(1, 8):
        cnt = K // (r1 + 1)
        ok = r8 < cnt
        cs.append(jnp.where(ok, s1[r1:r1 + 1] + s2[0:8], _NEG_INF))
        ci.append(i1[r1:r1 + 1] * PEER_N_KEYS + i2[0:8])
        pos.append(jnp.where(ok, r1 * K + r8, _BIG_I32 - 1))
    cs.append(s1[8:K] + s2[0:1])
    ci.append(i1[8:K] * PEER_N_KEYS + i2[0:1])
    pos.append((r8 + 8) * K)
    return jnp.concatenate(cs, axis=0), jnp.concatenate(ci, axis=0), jnp.concatenate(pos, axis=0)


def _peer_topk_kernel(h_ref, g_ref, wq_ref, keys_ref, hn_ref, idx_ref, gate_ref, q_scr, *, T):
    hd = pl.program_id(1)

    @pl.when(hd == 0)
    def _():
        hn = _rms(h_ref[...], g_ref[...])
        hn_ref[...] = hn
        q = _bdot(hn.astype(BF16), wq_ref[...]).astype(BF16)
        for j in range(2 * PEER_HEADS):
            q_scr[j] = q[:, j * PEER_HALF:(j + 1) * PEER_HALF]

    n_iota = lax.broadcasted_iota(jnp.int32, (PEER_N_KEYS, T), 0)
    tops = []
    for side in range(2):
        s = lax.dot_general(keys_ref[2 * hd + side], q_scr[2 * hd + side], (((1,), (1,)), ((), ())),
                            preferred_element_type=F32)
        tops.append(_topk_rows(s, n_iota, PEER_TOPK))
    cs, ci, pos = _peer_candidates(tops[0][0], tops[0][1], tops[1][0], tops[1][1], T)
    vals, experts = [], []
    for _ in range(PEER_TOPK):
        m = jnp.max(cs, axis=0, keepdims=True)
        p = jnp.min(jnp.where(cs == m, pos, _BIG_I32), axis=0, keepdims=True)
        sel = pos == p
        vals.append(m)
        experts.append(jnp.max(jnp.where(sel, ci, -1), axis=0, keepdims=True))
        cs = jnp.where(sel, _NEG_INF, cs)
    top_s = jnp.concatenate(vals, axis=0)
    e = jnp.exp(top_s - top_s[0:1])
    gate_ref[...] = e / jnp.sum(e, axis=0, keepdims=True)
    idx_ref[...] = jnp.concatenate(experts, axis=0)


def _peer_topk(h2, g, wq, keys, T):
    N = h2.shape[0]
    K = PEER_TOPK
    return pl.pallas_call(
        functools.partial(_peer_topk_kernel, T=T),
        grid=(N // T, PEER_HEADS),
        in_specs=[pl.BlockSpec((T, D_MODEL), lambda i, h: (i, 0)), _full(g.shape), _full(wq.shape),
                  _full(keys.shape)],
        out_specs=[pl.BlockSpec((T, D_MODEL), lambda i, h: (i, 0)),
                   pl.BlockSpec((K, T), lambda i, h: (h, i)),
                   pl.BlockSpec((K, T), lambda i, h: (h, i))],
        out_shape=[jax.ShapeDtypeStruct((N, D_MODEL), F32),
                   jax.ShapeDtypeStruct((PEER_HEADS * K, N), jnp.int32),
                   jax.ShapeDtypeStruct((PEER_HEADS * K, N), F32)],
        scratch_shapes=[pltpu.VMEM((2 * PEER_HEADS, T, PEER_HALF), BF16)],
        compiler_params=_cparams(("arbitrary", "arbitrary")),
        name="peer_topk",
    )(h2, g, wq, keys)


PEER_SLOTS = PEER_HEADS * PEER_TOPK
TABLE_ROWS = D_MODEL // (2 * LANES)
PEER_VMEM_LIMIT = 56 * 1024 * 1024


def _unpack_pair(w):
    lo = pltpu.bitcast(w << 16, F32)
    hi = pltpu.bitcast(w & jnp.int32(-65536), F32)
    return lo, hi


def _lane_sums_as_row(p):
    hi = p.astype(BF16)
    lo = (p - hi.astype(F32)).astype(BF16)
    ones = jnp.ones((SUBLANES, p.shape[1]), BF16)
    nt = (((1,), (1,)), ((), ()))
    r = (lax.dot_general(ones, hi, nt, preferred_element_type=F32)
         + lax.dot_general(ones, lo, nt, preferred_element_type=F32))
    return r[0:1]


def _peer_u_kernel(idx_ref, x_ref, gate_ref, tab_ref, w_ref, p_scr, *, tb):
    def token(t, carry):
        xt = x_ref[t]
        xlo = xt[:TABLE_ROWS]
        xhi = xt[TABLE_ROWS:]
        for k in range(PEER_SLOTS):
            row = pl.multiple_of(idx_ref[t, k] * TABLE_ROWS, TABLE_ROWS)
            lo, hi = _unpack_pair(tab_ref[pl.ds(row, TABLE_ROWS), :])
            p_scr[pl.ds(k, 1), :] = jnp.sum(lo * xlo + hi * xhi, axis=0, keepdims=True)
        act = _lane_sums_as_row(p_scr[...])
        w_ref[pl.ds(t, 1), :] = gate_ref[pl.ds(t, 1), :] * _gelu(act)
        return carry

    lax.fori_loop(0, tb, token, 0)


def _peer_v_kernel(idx_ref, w_ref, tab_ref, o_ref, *, tb):
    def token(t, carry):
        acc_lo = jnp.zeros((TABLE_ROWS, LANES), F32)
        acc_hi = jnp.zeros((TABLE_ROWS, LANES), F32)
        for k in range(PEER_SLOTS):
            row = pl.multiple_of(idx_ref[t, k] * TABLE_ROWS, TABLE_ROWS)
            lo, hi = _unpack_pair(tab_ref[pl.ds(row, TABLE_ROWS), :])
            wk = w_ref[t, k]
            acc_lo = acc_lo + wk * lo
            acc_hi = acc_hi + wk * hi
        o_ref[t] = jnp.concatenate([acc_lo, acc_hi], axis=0)
        return carry

    lax.fori_loop(0, tb, token, 0)


def _smem_tile(tb):
    return pl.BlockSpec((tb, PEER_SLOTS), lambda i: (i, 0), memory_space=pltpu.SMEM)


def _peer_cparams():
    return pltpu.CompilerParams(dimension_semantics=("arbitrary",), vmem_limit_bytes=PEER_VMEM_LIMIT)


def _peer_u(idx, x3, gates, tab, tb):
    N = idx.shape[0]
    return pl.pallas_call(
        functools.partial(_peer_u_kernel, tb=tb),
        grid=(N // tb,),
        in_specs=[_smem_tile(tb),
                  pl.BlockSpec((tb, SUBLANES, LANES), lambda i: (i, 0, 0)),
                  pl.BlockSpec((tb, PEER_SLOTS), lambda i: (i, 0)),
                  _full(tab.shape)],
        out_specs=pl.BlockSpec((tb, PEER_SLOTS), lambda i: (i, 0)),
        out_shape=jax.ShapeDtypeStruct((N, PEER_SLOTS), F32),
        scratch_shapes=[pltpu.VMEM((PEER_SLOTS, LANES), F32)],
        compiler_params=_peer_cparams(),
        name="peer_u_gather",
    )(idx, x3, gates, tab)


def _peer_v(idx, w, tab, tb):
    N = idx.shape[0]
    return pl.pallas_call(
        functools.partial(_peer_v_kernel, tb=tb),
        grid=(N // tb,),
        in_specs=[_smem_tile(tb), _smem_tile(tb), _full(tab.shape)],
        out_specs=pl.BlockSpec((tb, SUBLANES, LANES), lambda i: (i, 0, 0)),
        out_shape=jax.ShapeDtypeStruct((N, SUBLANES, LANES), F32),
        compiler_params=_peer_cparams(),
        name="peer_v_gather",
    )(idx, w, tab)


def _pack_table(t):
    E = t.shape[0]
    b = lax.bitcast_convert_type(t.astype(BF16), jnp.uint16).astype(jnp.uint32)
    b = b.reshape(E, 2, TABLE_ROWS, LANES)
    word = b[:, 0] | (b[:, 1] << 16)
    return lax.bitcast_convert_type(word, jnp.int32).reshape(E * TABLE_ROWS, LANES)


def _final_kernel(h_ref, p_ref, g_ref, o_ref):
    o_ref[...] = _rms(h_ref[...] + p_ref[...], g_ref[...])


def _final(h2, peer_out, g, tl):
    N = h2.shape[0]
    row = pl.BlockSpec((tl, D_MODEL), lambda i: (i, 0))
    return pl.pallas_call(
        _final_kernel,
        grid=(N // tl,),
        in_specs=[row, row, _full(g.shape)],
        out_specs=row,
        out_shape=jax.ShapeDtypeStruct(h2.shape, F32),
        compiler_params=_cparams(("arbitrary",)),
        name="final_norm",
    )(h2, peer_out, g)


def _peer(h2, g_peer, w_peer_q, sub_keys, u_emb, v_emb):
    N = h2.shape[0]
    keys = sub_keys.reshape(2 * PEER_HEADS, PEER_N_KEYS, PEER_HALF).astype(BF16)
    hn, idx_t, gate_t = _peer_topk(h2, g_peer[None], w_peer_q.astype(BF16), keys, min(128, N))
    idx = idx_t.T
    gates = gate_t.T
    tb = min(64, N)
    w = _peer_u(idx, hn.reshape(N, SUBLANES, LANES), gates, _pack_table(u_emb), tb)
    out = _peer_v(idx, w, _pack_table(v_emb), tb)
    return out.reshape(N, D_MODEL)


def _rope_swap(w):
    half = MLA_ROPE // 2
    return jnp.concatenate([w[..., half:], w[..., :half]], axis=-1)


def _pad_cols(w, width):
    return jnp.pad(w, ((0, 0), (0, width - w.shape[1])))


def _prep_mixer_weights(w_in, w_uq, w_ukv, w_mla_o):
    o0, o1, o2, o3 = SSM_WIDTH, SSM_WIDTH + MLA_Q_LORA, SSM_WIDTH + MLA_Q_LORA + MLA_KV_LORA, None
    o3 = o2 + MLA_ROPE
    w_kr = w_in[:, o2:o3]
    zl = jnp.zeros((D_MODEL, MLA_NOPE), F32)
    kr_pad = _pad_cols(jnp.concatenate([zl, w_kr], axis=1), HEAD_PAD)
    krs_pad = _pad_cols(jnp.concatenate([zl, _rope_swap(w_kr)], axis=1), HEAD_PAD)
    wa = jnp.concatenate([w_in[:, :o2], kr_pad, krs_pad], axis=1).astype(BF16)
    wg = w_in[:, o3:].astype(BF16)

    wq = w_uq.reshape(MLA_Q_LORA, MLA_HEADS, MLA_NOPE + MLA_ROPE)
    pad = HEAD_PAD - MLA_NOPE - MLA_ROPE
    wq_pad = jnp.pad(wq, ((0, 0), (0, 0), (0, pad)))
    wq_sw = jnp.concatenate([jnp.zeros_like(wq[..., :MLA_NOPE]), _rope_swap(wq[..., MLA_NOPE:])], axis=-1)
    wq_sw = jnp.pad(wq_sw, ((0, 0), (0, 0), (0, pad)))
    hw = MLA_HEADS * HEAD_PAD
    wq2 = jnp.concatenate([wq_pad.reshape(MLA_Q_LORA, hw), wq_sw.reshape(MLA_Q_LORA, hw)], axis=1).astype(BF16)

    wkv = w_ukv.reshape(MLA_KV_LORA, MLA_HEADS, MLA_NOPE + MLA_V)
    wk = jnp.pad(wkv[..., :MLA_NOPE], ((0, 0), (0, 0), (0, HEAD_PAD - MLA_NOPE))).reshape(MLA_KV_LORA, hw)
    wv = jnp.pad(wkv[..., MLA_NOPE:], ((0, 0), (0, 0), (0, HEAD_PAD - MLA_V))).reshape(MLA_KV_LORA, hw)
    wkv2 = jnp.concatenate([wk, wv], axis=1).astype(BF16)

    wmo = w_mla_o.reshape(MLA_HEADS, MLA_V, D_MODEL)
    wmo = jnp.pad(wmo, ((0, 0), (0, HEAD_PAD - MLA_V), (0, 0))).reshape(hw, D_MODEL).astype(BF16)
    return wa, wg, wq2, wkv2, wmo


def _rope_lane_consts():
    lane = jnp.arange(HEAD_PAD)
    half = MLA_ROPE // 2
    inv_freq = 1.0 / (ROPE_THETA ** (jnp.arange(0, MLA_ROPE, 2, dtype=F32) / MLA_ROPE))
    r = lane - MLA_NOPE
    in_rope = (r >= 0) & (r < MLA_ROPE)
    invf = jnp.where(in_rope, inv_freq[jnp.clip(r, 0, MLA_ROPE - 1) % half], 0.0)
    cmask = (lane < MLA_NOPE + MLA_ROPE).astype(F32)
    ssign = jnp.where(in_rope, jnp.where(r < half, -1.0, 1.0), 0.0)
    vone = (jnp.arange(MLA_HEADS * HEAD_PAD) % HEAD_PAD == MLA_V).astype(F32)
    return invf[None].astype(F32), cmask[None], ssign[None].astype(F32), vone[None]


def _s5_block_weights(bb_re, bb_im, c_re, c_im):
    gpc = SSM_CHUNK // SSM_GROUP
    eye = jnp.eye(gpc, dtype=F32)

    def bblock(bb):
        bb = bb.reshape(2, SSM_CHUNKS, gpc, SSM_STATE, SSM_GROUP)
        return jnp.einsum('ab,djapc->djacbp', eye, bb).reshape(2, SSM_CHUNKS, SSM_CHUNK, SSM_CHUNK_STATES)

    def cblock(c):
        c = c.reshape(2, SSM_CHUNKS, gpc, SSM_GROUP, SSM_STATE)
        return jnp.einsum('ab,djacp->djbpac', eye, c).reshape(2, SSM_CHUNKS, SSM_CHUNK_STATES, SSM_CHUNK)

    bm = jnp.concatenate([bblock(bb_re), bblock(bb_im)], axis=-1).astype(BF16)
    cm = jnp.concatenate([cblock(c_re), -cblock(c_im)], axis=-2).astype(BF16)
    return bm, cm


def _front(l, h, mem, positions, g_mix, w_in, b_gate, ssm_lam_re, ssm_lam_im, ssm_log_dt, ssm_b_re,
           ssm_b_im, ssm_c_re, ssm_c_im, ssm_d, w_ssm_glu, g_q_lora, w_uq, g_kv_lora, w_ukv, w_mla_o,
           w_out, g_mem_q, g_mem_kv, w_mem_q, w_mem_kv, w_mem_o, B, L):
    M = mem.shape[1]
    N = B * L
    tl = min(512, L)
    pos2 = positions.reshape(N, 1)
    invf, cmask, ssign, vone = _rope_lane_consts()
    wa, wg, wq2, wkv2, wmo = _prep_mixer_weights(w_in[l], w_uq[l], w_ukv[l], w_mla_o[l])
    u_tm, q, k, v, gates = _in_proj(
        h, pos2, g_mix[l][None], wa, wg, b_gate[l][None], g_q_lora[l][None], wq2,
        g_kv_lora[l][None], wkv2, invf, cmask, ssign, vone, B, L, tl)
    ar, ai, bbr, bbi = _s5_discretise(ssm_lam_re[l], ssm_lam_im[l], ssm_log_dt[l],
                                      ssm_b_re[l], ssm_b_im[l])
    bm, cm = _s5_block_weights(bbr, bbi, ssm_c_re[l], ssm_c_im[l])
    u_rows = u_tm.reshape(L * B, SSM_WIDTH)
    yf, yb = _s5(u_rows, bm, cm, ar.reshape(2, 1, SSM_STATES), ai.reshape(2, 1, SSM_STATES),
                 B, L, min(32, L))
    attn = _attention(q, k, v, B, L, min(512, L))
    h = _merge(h, yf.reshape(L, B * SSM_WIDTH), yb.reshape(L, B * SSM_WIDTH), u_tm, attn, gates,
               ssm_d[l][None], w_ssm_glu[l].astype(BF16), wmo, w_out[l].astype(BF16), B, L, tl)
    kv_mem = _mem_kv(mem.reshape(B * M, D_MODEL), g_mem_kv[l][None], w_mem_kv[l].astype(BF16), M)
    return _mem_attn(h, kv_mem, g_mem_q[l][None], w_mem_q[l].astype(BF16), w_mem_o[l].astype(BF16),
                     B, L, M, tl)


def kernel(x, mem, positions, g_mix, w_in, b_gate, ssm_lam_re, ssm_lam_im, ssm_log_dt, ssm_b_re, ssm_b_im, ssm_c_re, ssm_c_im, ssm_d, w_ssm_glu, g_q_lora, w_uq, g_kv_lora, w_ukv, w_mla_o, w_out, g_mem_q, g_mem_kv, w_mem_q, w_mem_kv, w_mem_o, g_peer, w_peer_q, peer_sub_keys, peer_u, peer_v, g_final):
    B, L, _ = x.shape
    depth = g_mix.shape[0]
    h = x.reshape(B * L, D_MODEL)
    for l in range(depth):
        h = _front(l, h, mem, positions, g_mix, w_in, b_gate, ssm_lam_re, ssm_lam_im, ssm_log_dt,
                   ssm_b_re, ssm_b_im, ssm_c_re, ssm_c_im, ssm_d, w_ssm_glu, g_q_lora, w_uq, g_kv_lora,
                   w_ukv, w_mla_o, w_out, g_mem_q, g_mem_kv, w_mem_q, w_mem_kv, w_mem_o, B, L)
        p = _peer(h, g_peer[l], w_peer_q[l], peer_sub_keys[l], peer_u[l], peer_v[l])
        if l + 1 < depth:
            h = h + p
    return _final(h, p, g_final[None], min(512, L)).reshape(B, L, D_MODEL)
```

```python
import functools
import math

import jax
import jax.numpy as jnp
from jax import lax
from jax.experimental import pallas as pl
from jax.experimental.pallas import tpu as pltpu

F32 = jnp.float32
BF16 = jnp.bfloat16

D_MODEL = 1024
SSM_WIDTH = 512
SSM_GROUP = 16
SSM_GROUPS = 32
SSM_STATE = 64
MLA_HEADS = 8
MLA_NOPE = 64
MLA_ROPE = 32
MLA_V = 64
MLA_Q_LORA = 384
MLA_KV_LORA = 256
ROPE_THETA = 10000.0
MEM_HEADS = 4
MEM_HEAD_DIM = 256
PEER_HEADS = 8
PEER_N_KEYS = 128
PEER_HALF = 128
PEER_TOPK = 16
NORM_EPS = 1e-6

LANES = 128
SUBLANES = 8
VMEM_LIMIT = 48 * 1024 * 1024

HEAD_PAD = LANES
SSM_CHUNK = LANES
SSM_CHUNKS = SSM_WIDTH // SSM_CHUNK
SSM_CHUNK_STATES = (SSM_CHUNK // SSM_GROUP) * SSM_STATE
SSM_STATES = SSM_GROUPS * SSM_STATE


def _cparams(sem):
    return pltpu.CompilerParams(dimension_semantics=sem, vmem_limit_bytes=VMEM_LIMIT)


def _rms(x, g):
    return x * lax.rsqrt(jnp.mean(x * x, axis=-1, keepdims=True) + NORM_EPS) * g


def _bdot(a, b):
    return jnp.dot(a, b, preferred_element_type=F32)


def _gelu(x):
    return jax.nn.gelu(x, approximate=True)


def _full(shape):
    nd = len(shape)
    return pl.BlockSpec(shape, lambda *_: (0,) * nd)


def _in_proj_kernel(x_ref, pos_ref, gmix_ref, wa_ref, wg_ref, bg_ref, gq_ref, wq_ref,
                    gkv_ref, wkv_ref, invf_ref, cmask_ref, ssign_ref, vone_ref,
                    u_ref, q_ref, k_ref, v_ref, gate_ref):
    xn = _rms(x_ref[...], gmix_ref[...]).astype(BF16)
    proj = _bdot(xn, wa_ref[...])
    u_ref[...] = proj[:, :SSM_WIDTH]
    gate_ref[...] = jax.nn.sigmoid(_bdot(xn, wg_ref[...]) + bg_ref[...]).astype(BF16)

    ang = pos_ref[...].astype(F32) * invf_ref[...]
    cosf = jnp.cos(ang) * cmask_ref[...]
    sinf = jnp.sin(ang) * ssign_ref[...]

    o = SSM_WIDTH
    cq = _rms(proj[:, o:o + MLA_Q_LORA], gq_ref[...]).astype(BF16)
    o += MLA_Q_LORA
    ckv = _rms(proj[:, o:o + MLA_KV_LORA], gkv_ref[...]).astype(BF16)
    o += MLA_KV_LORA
    kr = proj[:, o:o + HEAD_PAD]
    krs = proj[:, o + HEAD_PAD:o + 2 * HEAD_PAD]
    kr_rot = kr * cosf + krs * sinf

    qq = _bdot(cq, wq_ref[...])
    kv = _bdot(ckv, wkv_ref[...])
    scale = (MLA_NOPE + MLA_ROPE) ** -0.5 * math.log2(math.e)
    hw = MLA_HEADS * HEAD_PAD
    for h in range(MLA_HEADS):
        sl = slice(h * HEAD_PAD, (h + 1) * HEAD_PAD)
        sl2 = slice(hw + h * HEAD_PAD, hw + (h + 1) * HEAD_PAD)
        q_ref[:, sl] = ((qq[:, sl] * cosf + qq[:, sl2] * sinf) * scale).astype(BF16)
        k_ref[:, sl] = (kv[:, sl] + kr_rot).astype(BF16)
    v_ref[...] = (kv[:, hw:] + vone_ref[...]).astype(BF16)


def _in_proj(x2, pos2, gmix, wa, wg, bg, gq, wq2, gkv, wkv2, invf, cmask, ssign, vone, B, L, tl):
    N = B * L
    nl = L // tl
    hw = MLA_HEADS * HEAD_PAD
    row = lambda b, i: (b * nl + i, 0)
    consts = [gmix, wa, wg, bg, gq, wq2, gkv, wkv2, invf, cmask, ssign, vone]
    return pl.pallas_call(
        _in_proj_kernel,
        grid=(B, nl),
        in_specs=[pl.BlockSpec((tl, D_MODEL), row), pl.BlockSpec((tl, 1), row)]
        + [_full(c.shape) for c in consts],
        out_specs=[pl.BlockSpec((tl, SSM_WIDTH), lambda b, i: (i, b)),
                   pl.BlockSpec((tl, hw), row), pl.BlockSpec((tl, hw), row),
                   pl.BlockSpec((tl, hw), row), pl.BlockSpec((tl, 2 * D_MODEL), row)],
        out_shape=[jax.ShapeDtypeStruct((L, B * SSM_WIDTH), F32),
                   jax.ShapeDtypeStruct((N, hw), BF16), jax.ShapeDtypeStruct((N, hw), BF16),
                   jax.ShapeDtypeStruct((N, hw), BF16), jax.ShapeDtypeStruct((N, 2 * D_MODEL), BF16)],
        compiler_params=_cparams(("arbitrary", "arbitrary")),
        name="in_proj",
    )(x2, pos2, *consts)


def _s5_disc_kernel(lr_ref, li_ref, ldt_ref, bre_ref, bim_ref, ar_ref, ai_ref, bbr_ref, bbi_ref):
    lr = lr_ref[...]
    li = li_ref[...]
    dt = jnp.exp(ldt_ref[...])
    mag = jnp.exp(lr * dt)
    ar = mag * jnp.cos(li * dt)
    ai = mag * jnp.sin(li * dt)
    nr = ar - 1.0
    den = lr * lr + li * li
    fr = (nr * lr + ai * li) / den
    fi = (ai * lr - nr * li) / den
    ar_ref[...] = ar
    ai_ref[...] = ai
    bre = bre_ref[...]
    bim = bim_ref[...]
    bbr_ref[...] = fr * bre - fi * bim
    bbi_ref[...] = fr * bim + fi * bre


def _s5_discretise(lam_re, lam_im, log_dt, b_re, b_im):
    n = lam_re.size
    ldt = jnp.broadcast_to(log_dt[..., None], lam_re.shape).reshape(n, 1)
    sds = jax.ShapeDtypeStruct
    ar, ai, bbr, bbi = pl.pallas_call(
        _s5_disc_kernel,
        out_shape=[sds((n, 1), F32), sds((n, 1), F32), sds((n, SSM_GROUP), F32), sds((n, SSM_GROUP), F32)],
        name="s5_discretise",
    )(lam_re.reshape(n, 1), lam_im.reshape(n, 1), ldt,
      b_re.reshape(n, SSM_GROUP), b_im.reshape(n, SSM_GROUP))
    return (ar.reshape(lam_re.shape), ai.reshape(lam_re.shape),
            bbr.reshape(b_re.shape), bbi.reshape(b_re.shape))


def _s5_kernel(uf_ref, ub_ref, bm_ref, cm_ref, ar_ref, ai_ref, yf_ref, yb_ref,
               st_ref, sf_ref, sb_ref, *, tt, nb):
    S = SSM_STATES
    CS = SSM_CHUNK_STATES

    @pl.when(pl.program_id(0) == 0)
    def _():
        st_ref[...] = jnp.zeros_like(st_ref)

    for d, (u_ref, s_ref) in enumerate(((uf_ref, sf_ref), (ub_ref, sb_ref))):
        ub = u_ref[...].astype(BF16)
        for j in range(SSM_CHUNKS):
            bu = _bdot(ub[:, j * SSM_CHUNK:(j + 1) * SSM_CHUNK], bm_ref[d, j])
            s_ref[:, j * CS:(j + 1) * CS] = bu[:, :CS]
            s_ref[:, S + j * CS:S + (j + 1) * CS] = bu[:, CS:]

    for c in range(S // CS):
        lre = slice(c * CS, (c + 1) * CS)
        lim = slice(S + c * CS, S + (c + 1) * CS)
        arf = jnp.broadcast_to(ar_ref[0, :, lre], (nb, CS))
        aif = jnp.broadcast_to(ai_ref[0, :, lre], (nb, CS))
        arb = jnp.broadcast_to(ar_ref[1, :, lre], (nb, CS))
        aib = jnp.broadcast_to(ai_ref[1, :, lre], (nb, CS))

        def body(t, carry):
            fr, fi, br, bi = carry
            rf = pl.multiple_of(t * nb, nb)
            rb = pl.multiple_of((tt - 1 - t) * nb, nb)
            nfr = arf * fr - aif * fi + sf_ref[pl.ds(rf, nb), lre]
            nfi = arf * fi + aif * fr + sf_ref[pl.ds(rf, nb), lim]
            sf_ref[pl.ds(rf, nb), lre] = nfr
            sf_ref[pl.ds(rf, nb), lim] = nfi
            nbr = arb * br - aib * bi + sb_ref[pl.ds(rb, nb), lre]
            nbi = arb * bi + aib * br + sb_ref[pl.ds(rb, nb), lim]
            sb_ref[pl.ds(rb, nb), lre] = nbr
            sb_ref[pl.ds(rb, nb), lim] = nbi
            return nfr, nfi, nbr, nbi

        init = (st_ref[0, :, lre], st_ref[0, :, lim], st_ref[1, :, lre], st_ref[1, :, lim])
        fr, fi, br, bi = lax.fori_loop(0, tt, body, init)
        st_ref[0, :, lre] = fr
        st_ref[0, :, lim] = fi
        st_ref[1, :, lre] = br
        st_ref[1, :, lim] = bi

    for d, (s_ref, y_ref) in enumerate(((sf_ref, yf_ref), (sb_ref, yb_ref))):
        for j in range(SSM_CHUNKS):
            sre = s_ref[:, j * CS:(j + 1) * CS].astype(BF16)
            sim = s_ref[:, S + j * CS:S + (j + 1) * CS].astype(BF16)
            y_ref[:, j * SSM_CHUNK:(j + 1) * SSM_CHUNK] = (
                _bdot(sre, cm_ref[d, j, :CS, :]) + _bdot(sim, cm_ref[d, j, CS:, :]))


def _s5(u_tm, bm, cm, ar, ai, B, L, tt):
    nblk = L // tt
    R = tt * B
    blk = lambda idx: pl.BlockSpec((R, SSM_WIDTH), idx)
    fwd = lambda i: (i, 0)
    bwd = lambda i: (nblk - 1 - i, 0)
    return pl.pallas_call(
        functools.partial(_s5_kernel, tt=tt, nb=B),
        grid=(nblk,),
        in_specs=[blk(fwd), blk(bwd), _full(bm.shape), _full(cm.shape), _full(ar.shape), _full(ai.shape)],
        out_specs=[blk(fwd), blk(bwd)],
        out_shape=[jax.ShapeDtypeStruct((L * B, SSM_WIDTH), F32)] * 2,
        scratch_shapes=[pltpu.VMEM((2, B, 2 * SSM_STATES), F32),
                        pltpu.VMEM((R, 2 * SSM_STATES), F32),
                        pltpu.VMEM((R, 2 * SSM_STATES), F32)],
        compiler_params=_cparams(("arbitrary",)),
        name="s5_scan",
    )(u_tm, u_tm, bm, cm, ar, ai)


def _attn_kernel(q_ref, k_ref, v_ref, o_ref, *, ck):
    q = q_ref[...]
    tq = q.shape[0]
    m = jnp.full((tq, 1), _NEG_INF, F32)
    acc = jnp.zeros((tq, HEAD_PAD), F32)
    for j in range(k_ref.shape[0] // ck):
        s = lax.dot_general(q, k_ref[j * ck:(j + 1) * ck, :], (((1,), (1,)), ((), ())),
                            preferred_element_type=F32)
        m_new = jnp.maximum(m, jnp.max(s, axis=-1, keepdims=True))
        p = jnp.exp2(s - m_new).astype(BF16)
        acc = acc * jnp.exp2(m - m_new) + _bdot(p, v_ref[j * ck:(j + 1) * ck, :])
        m = m_new
    o_ref[...] = (acc / acc[:, MLA_V:MLA_V + 1]).astype(BF16)


def _attention(q, k, v, B, L, tq, ck):
    nq = L // tq
    return pl.pallas_call(
        functools.partial(_attn_kernel, ck=ck),
        grid=(B, MLA_HEADS, nq),
        in_specs=[pl.BlockSpec((tq, HEAD_PAD), lambda b, h, i: (b * nq + i, h)),
                  pl.BlockSpec((L, HEAD_PAD), lambda b, h, i: (b, h)),
                  pl.BlockSpec((L, HEAD_PAD), lambda b, h, i: (b, h))],
        out_specs=pl.BlockSpec((tq, HEAD_PAD), lambda b, h, i: (b * nq + i, h)),
        out_shape=jax.ShapeDtypeStruct(q.shape, BF16),
        compiler_params=_cparams(("arbitrary", "arbitrary", "arbitrary")),
        name="mla_attention",
    )(q, k, v)


def _merge_kernel(x_ref, yf_ref, yb_ref, u_ref, at_ref, gate_ref, d_ref, wglu_ref, wmo_ref, wout_ref,
                  h_ref):
    y = yf_ref[...] + yb_ref[...] + d_ref[...] * u_ref[...]
    y = _gelu(y).astype(BF16)
    glu = _bdot(y, wglu_ref[...])
    branch_ssm = glu[:, :D_MODEL] * jax.nn.sigmoid(glu[:, D_MODEL:])
    branch_mla = _bdot(at_ref[...], wmo_ref[...])
    g = gate_ref[...].astype(F32)
    merged = (g[:, :D_MODEL] * branch_ssm + g[:, D_MODEL:] * branch_mla).astype(BF16)
    h_ref[...] = x_ref[...] + _bdot(merged, wout_ref[...])


def _merge(x2, yf_tm, yb_tm, u_tm, attn, gates, d_skip, wglu, wmo, wout, B, L, tl):
    nl = L // tl
    row = lambda b, i: (b * nl + i, 0)
    tm = pl.BlockSpec((tl, SSM_WIDTH), lambda b, i: (i, b))
    consts = [d_skip, wglu, wmo, wout]
    return pl.pallas_call(
        _merge_kernel,
        grid=(B, nl),
        in_specs=[pl.BlockSpec((tl, D_MODEL), row), tm, tm, tm,
                  pl.BlockSpec((tl, attn.shape[1]), row), pl.BlockSpec((tl, 2 * D_MODEL), row)]
        + [_full(c.shape) for c in consts],
        out_specs=pl.BlockSpec((tl, D_MODEL), row),
        out_shape=jax.ShapeDtypeStruct(x2.shape, F32),
        compiler_params=_cparams(("arbitrary", "arbitrary")),
        name="merge_out",
    )(x2, yf_tm, yb_tm, u_tm, attn, gates, *consts)


def _mem_kv_kernel(mem_ref, g_ref, w_ref, kv_ref):
    mn = _rms(mem_ref[...], g_ref[...]).astype(BF16)
    kv_ref[...] = _bdot(mn, w_ref[...]).astype(BF16)


def _mem_kv(mem2, g, w, tm):
    n = mem2.shape[0]
    return pl.pallas_call(
        _mem_kv_kernel,
        grid=(n // tm,),
        in_specs=[pl.BlockSpec((tm, D_MODEL), lambda i: (i, 0)), _full(g.shape), _full(w.shape)],
        out_specs=pl.BlockSpec((tm, 2 * D_MODEL), lambda i: (i, 0)),
        out_shape=jax.ShapeDtypeStruct((n, 2 * D_MODEL), BF16),
        compiler_params=_cparams(("arbitrary",)),
        name="mem_kv",
    )(mem2, g, w)


def _mem_attn_kernel(h_ref, kv_ref, g_ref, wq_ref, wo_ref, o_ref):
    h = h_ref[...]
    q = _bdot(_rms(h, g_ref[...]).astype(BF16), wq_ref[...])
    scale = MEM_HEAD_DIM ** -0.5
    outs = []
    for hd in range(MEM_HEADS):
        sl = slice(hd * MEM_HEAD_DIM, (hd + 1) * MEM_HEAD_DIM)
        kh = kv_ref[:, sl]
        vh = kv_ref[:, D_MODEL + hd * MEM_HEAD_DIM:D_MODEL + (hd + 1) * MEM_HEAD_DIM]
        s = lax.dot_general(q[:, sl].astype(BF16), kh, (((1,), (1,)), ((), ())),
                            preferred_element_type=F32) * scale
        p = jnp.exp(s - jnp.max(s, axis=-1, keepdims=True))
        p = p / jnp.sum(p, axis=-1, keepdims=True)
        outs.append(_bdot(p.astype(BF16), vh).astype(BF16))
    o = jnp.concatenate(outs, axis=-1)
    o_ref[...] = h + _bdot(o, wo_ref[...])


def _mem_attn(h2, kv, g, wq, wo, B, L, M, tl):
    nl = L // tl
    row = lambda b, i: (b * nl + i, 0)
    consts = [g, wq, wo]
    return pl.pallas_call(
        _mem_attn_kernel,
        grid=(B, nl),
        in_specs=[pl.BlockSpec((tl, D_MODEL), row), pl.BlockSpec((M, 2 * D_MODEL), lambda b, i: (b, 0))]
        + [_full(c.shape) for c in consts],
        out_specs=pl.BlockSpec((tl, D_MODEL), row),
        out_shape=jax.ShapeDtypeStruct(h2.shape, F32),
        compiler_params=_cparams(("arbitrary", "arbitrary")),
        name="mem_attention",
    )(h2, kv, *consts)


_NEG_INF = float("-inf")
_BIG = 3.0e38
PEER_EXPERTS = PEER_N_KEYS * PEER_N_KEYS


def _topk_rows(problems, k):
    ss = [s for s, _ in problems]
    vals = [[] for _ in problems]
    pays = [[] for _ in problems]
    for _ in range(k):
        for i, (_, payload) in enumerate(problems):
            m = jnp.max(ss[i], axis=0, keepdims=True)
            cand = jnp.where(ss[i] == m, payload, _BIG)
            p = jnp.min(cand, axis=0, keepdims=True)
            vals[i].append(m)
            pays[i].append(p)
            ss[i] = jnp.where(cand == p, _NEG_INF, ss[i])
    return [(jnp.concatenate(v, axis=0), jnp.concatenate(p, axis=0)) for v, p in zip(vals, pays)]


def _peer_candidates(s1, i1, s2, i2, T):
    K = PEER_TOPK
    E = float(PEER_EXPERTS)
    r8 = lax.broadcasted_iota(jnp.int32, (8, T), 0)
    r8f = r8.astype(F32)
    r16f = lax.broadcasted_iota(jnp.int32, (K, T), 0).astype(F32)
    cs = [s1[0:1] + s2]
    pay = [r16f * E + (i1[0:1] * PEER_N_KEYS + i2)]
    for r1 in range(1, 8):
        ok = r8 < K // (r1 + 1)
        cs.append(jnp.where(ok, s1[r1:r1 + 1] + s2[0:8], _NEG_INF))
        pay.append((r8f + r1 * K) * E + (i1[r1:r1 + 1] * PEER_N_KEYS + i2[0:8]))
    cs.append(s1[8:K] + s2[0:1])
    pay.append((r8f + 8.0) * (K * E) + (i1[8:K] * PEER_N_KEYS + i2[0:1]))
    return jnp.concatenate(cs, axis=0), jnp.concatenate(pay, axis=0)


def _peer_topk_kernel(h_ref, g_ref, wq_ref, keys_ref, hn_ref, idx_ref, gate_ref, q_scr, *, T, hp):
    hd = pl.program_id(1)

    @pl.when(hd == 0)
    def _():
        hn = _rms(h_ref[...], g_ref[...])
        hn_ref[...] = hn
        q = _bdot(hn.astype(BF16), wq_ref[...]).astype(BF16)
        for j in range(2 * PEER_HEADS):
            q_scr[j] = q[:, j * PEER_HALF:(j + 1) * PEER_HALF]

    n_iota = lax.broadcasted_iota(jnp.int32, (PEER_N_KEYS, T), 0).astype(F32)
    K = PEER_TOPK
    problems = []
    for j in range(2 * hp):
        hs = 2 * hp * hd + j
        s = lax.dot_general(keys_ref[hs], q_scr[hs], (((1,), (1,)), ((), ())),
                            preferred_element_type=F32)
        problems.append((s, n_iota))
    tops = _topk_rows(problems, K)
    cands = [_peer_candidates(*tops[2 * i], *tops[2 * i + 1], T) for i in range(hp)]
    for i, (top_s, top_pay) in enumerate(_topk_rows(cands, K)):
        expert = top_pay - jnp.floor(top_pay * (1.0 / PEER_EXPERTS)) * PEER_EXPERTS
        e = jnp.exp(top_s - top_s[0:1])
        gate_ref[i * K:(i + 1) * K, :] = e / jnp.sum(e, axis=0, keepdims=True)
        idx_ref[i * K:(i + 1) * K, :] = expert.astype(jnp.int32) * TABLE_ROWS


def _peer_topk(h2, g, wq, keys, T, hp):
    N = h2.shape[0]
    K = PEER_TOPK * hp
    return pl.pallas_call(
        functools.partial(_peer_topk_kernel, T=T, hp=hp),
        grid=(N // T, PEER_HEADS // hp),
        in_specs=[pl.BlockSpec((T, D_MODEL), lambda i, h: (i, 0)), _full(g.shape), _full(wq.shape),
                  _full(keys.shape)],
        out_specs=[pl.BlockSpec((T, D_MODEL), lambda i, h: (i, 0)),
                   pl.BlockSpec((K, T), lambda i, h: (h, i)),
                   pl.BlockSpec((K, T), lambda i, h: (h, i))],
        out_shape=[jax.ShapeDtypeStruct((N, D_MODEL), F32),
                   jax.ShapeDtypeStruct((PEER_SLOTS, N), jnp.int32),
                   jax.ShapeDtypeStruct((PEER_SLOTS, N), F32)],
        scratch_shapes=[pltpu.VMEM((2 * PEER_HEADS, T, PEER_HALF), BF16)],
        compiler_params=_cparams(("arbitrary", "arbitrary")),
        name="peer_topk",
    )(h2, g, wq, keys)


PEER_SLOTS = PEER_HEADS * PEER_TOPK
TABLE_ROWS = D_MODEL // (2 * LANES)
PEER_VMEM_LIMIT = 56 * 1024 * 1024


GATHER_ROWS = PEER_SLOTS * TABLE_ROWS
GATHER_BF16_ROWS = 2 * GATHER_ROWS


def _chunk_mask():
    shape = (SUBLANES, GATHER_BF16_ROWS)
    c = lax.broadcasted_iota(jnp.int32, shape, 0)
    j = lax.broadcasted_iota(jnp.int32, shape, 1) & (2 * TABLE_ROWS - 1)
    return c == (j >> 1) + TABLE_ROWS * (j & 1)


def _split_bf16(x):
    hi = x.astype(BF16)
    return hi, (x - hi.astype(F32)).astype(BF16)


def _gather_group(idx_ref, tab_ref, g_buf, t0):
    for j in range(SUBLANES):
        for k in range(PEER_SLOTS):
            row = pl.multiple_of(idx_ref[t0 + j, k], TABLE_ROWS)
            g_buf[j, k * TABLE_ROWS:(k + 1) * TABLE_ROWS, :] = tab_ref[pl.ds(row, TABLE_ROWS), :]


def _pipelined_groups(idx_ref, tab_ref, g_a, g_b, tb, consume):
    G = SUBLANES
    last = tb - G
    _gather_group(idx_ref, tab_ref, g_a, 0)

    def pair(p, carry):
        ta = pl.multiple_of(p * 2 * G, G)
        tb_ = ta + G
        _gather_group(idx_ref, tab_ref, g_b, tb_)
        consume(g_a, ta)
        _gather_group(idx_ref, tab_ref, g_a, jnp.minimum(ta + 2 * G, last))
        consume(g_b, tb_)
        return carry

    lax.fori_loop(0, tb // (2 * G), pair, 0)


def _peer_u_kernel(idx_ref, x_ref, gate_ref, tab_ref, e8_ref, w_ref, g_a, g_b, *, tb):
    mask = _chunk_mask()
    rows = lax.broadcasted_iota(jnp.int32, (SUBLANES, GATHER_BF16_ROWS), 0)
    nt = (((1,), (1,)), ((), ()))

    def consume(g_buf, t0):
        dall = jnp.zeros((SUBLANES, GATHER_BF16_ROWS), F32)
        for j in range(SUBLANES):
            gb = pltpu.bitcast(g_buf[j], BF16)
            r = lax.dot_general(x_ref[t0 + j].astype(BF16), gb, nt,
                                preferred_element_type=F32)
            d = jnp.sum(jnp.where(mask, r, 0.0), axis=0, keepdims=True)
            dall = jnp.where(rows == j, d, dall)
        hi, lo = _split_bf16(dall)
        act = _bdot(hi, e8_ref[...]) + _bdot(lo, e8_ref[...])
        w_ref[pl.ds(t0, SUBLANES), :] = gate_ref[pl.ds(t0, SUBLANES), :] * _gelu(act)

    _pipelined_groups(idx_ref, tab_ref, g_a, g_b, tb, consume)


def _peer_v_kernel(idx_ref, w_ref, tab_ref, e8t_ref, o_ref, g_a, g_b, *, tb):
    mask = _chunk_mask()
    shape = (SUBLANES, GATHER_BF16_ROWS)

    def consume(g_buf, t0):
        hi, lo = _split_bf16(w_ref[pl.ds(t0, SUBLANES), :])
        rep_hi = _bdot(hi, e8t_ref[...])
        rep_lo = _bdot(lo, e8t_ref[...])
        for j in range(SUBLANES):
            gb = pltpu.bitcast(g_buf[j], BF16)
            whi = jnp.where(mask, jnp.broadcast_to(rep_hi[j:j + 1], shape), 0.0).astype(BF16)
            wlo = jnp.where(mask, jnp.broadcast_to(rep_lo[j:j + 1], shape), 0.0).astype(BF16)
            o_ref[t0 + j] = _bdot(whi, gb) + _bdot(wlo, gb)

    _pipelined_groups(idx_ref, tab_ref, g_a, g_b, tb, consume)


def _smem_tile(tb):
    return pl.BlockSpec((tb, PEER_SLOTS), lambda i: (i, 0), memory_space=pltpu.SMEM)


def _peer_cparams():
    return pltpu.CompilerParams(dimension_semantics=("arbitrary",), vmem_limit_bytes=PEER_VMEM_LIMIT)


def _slot_expand():
    return jnp.repeat(jnp.eye(PEER_SLOTS, dtype=BF16), 2 * TABLE_ROWS, axis=0)


def _peer_u(idx4, x3, gates, tab, tb):
    N = idx4.shape[0]
    e8 = _slot_expand()
    tile = pl.BlockSpec((tb, PEER_SLOTS), lambda i: (i, 0))
    return pl.pallas_call(
        functools.partial(_peer_u_kernel, tb=tb),
        grid=(N // tb,),
        in_specs=[_smem_tile(tb), pl.BlockSpec((tb, SUBLANES, LANES), lambda i: (i, 0, 0)), tile,
                  _full(tab.shape), _full(e8.shape)],
        out_specs=tile,
        out_shape=jax.ShapeDtypeStruct((N, PEER_SLOTS), F32),
        scratch_shapes=[pltpu.VMEM((SUBLANES, GATHER_ROWS, LANES), jnp.int32)] * 2,
        compiler_params=_peer_cparams(),
        name="peer_u_gather",
    )(idx4, x3, gates, tab, e8)


def _peer_v(idx4, w, tab, tb):
    N = idx4.shape[0]
    e8t = _slot_expand().T
    return pl.pallas_call(
        functools.partial(_peer_v_kernel, tb=tb),
        grid=(N // tb,),
        in_specs=[_smem_tile(tb), pl.BlockSpec((tb, PEER_SLOTS), lambda i: (i, 0)),
                  _full(tab.shape), _full(e8t.shape)],
        out_specs=pl.BlockSpec((tb, SUBLANES, LANES), lambda i: (i, 0, 0)),
        out_shape=jax.ShapeDtypeStruct((N, SUBLANES, LANES), F32),
        scratch_shapes=[pltpu.VMEM((SUBLANES, GATHER_ROWS, LANES), jnp.int32)] * 2,
        compiler_params=_peer_cparams(),
        name="peer_v_gather",
    )(idx4, w, tab, e8t)


def _pack_table(t):
    E = t.shape[0]
    b = lax.bitcast_convert_type(t.astype(BF16), jnp.uint16).astype(jnp.uint32)
    b = b.reshape(E, 2, TABLE_ROWS, LANES)
    word = b[:, 0] | (b[:, 1] << 16)
    return lax.bitcast_convert_type(word, jnp.int32).reshape(E * TABLE_ROWS, LANES)


def _final_kernel(h_ref, p_ref, g_ref, o_ref):
    o_ref[...] = _rms(h_ref[...] + p_ref[...], g_ref[...])


def _final(h2, peer_out, g, tl):
    N = h2.shape[0]
    row = pl.BlockSpec((tl, D_MODEL), lambda i: (i, 0))
    return pl.pallas_call(
        _final_kernel,
        grid=(N // tl,),
        in_specs=[row, row, _full(g.shape)],
        out_specs=row,
        out_shape=jax.ShapeDtypeStruct(h2.shape, F32),
        compiler_params=_cparams(("arbitrary",)),
        name="final_norm",
    )(h2, peer_out, g)


def _peer(h2, g_peer, w_peer_q, sub_keys, u_emb, v_emb):
    N = h2.shape[0]
    keys = sub_keys.reshape(2 * PEER_HEADS, PEER_N_KEYS, PEER_HALF).astype(BF16)
    hn, idx_t, gate_t = _peer_topk(h2, g_peer[None], w_peer_q.astype(BF16), keys, min(256, N), 4)
    idx4 = idx_t.T
    gates = gate_t.T
    tb = min(128, N)
    w = _peer_u(idx4, hn.reshape(N, SUBLANES, LANES), gates, _pack_table(u_emb), tb)
    out = _peer_v(idx4, w, _pack_table(v_emb), tb)
    return out.reshape(N, D_MODEL)


def _rope_swap(w):
    half = MLA_ROPE // 2
    return jnp.concatenate([w[..., half:], w[..., :half]], axis=-1)


def _pad_cols(w, width):
    return jnp.pad(w, ((0, 0), (0, width - w.shape[1])))


def _prep_mixer_weights(w_in, w_uq, w_ukv, w_mla_o):
    o0, o1, o2, o3 = SSM_WIDTH, SSM_WIDTH + MLA_Q_LORA, SSM_WIDTH + MLA_Q_LORA + MLA_KV_LORA, None
    o3 = o2 + MLA_ROPE
    w_kr = w_in[:, o2:o3]
    zl = jnp.zeros((D_MODEL, MLA_NOPE), F32)
    kr_pad = _pad_cols(jnp.concatenate([zl, w_kr], axis=1), HEAD_PAD)
    krs_pad = _pad_cols(jnp.concatenate([zl, _rope_swap(w_kr)], axis=1), HEAD_PAD)
    wa = jnp.concatenate([w_in[:, :o2], kr_pad, krs_pad], axis=1).astype(BF16)
    wg = w_in[:, o3:].astype(BF16)

    wq = w_uq.reshape(MLA_Q_LORA, MLA_HEADS, MLA_NOPE + MLA_ROPE)
    pad = HEAD_PAD - MLA_NOPE - MLA_ROPE
    wq_pad = jnp.pad(wq, ((0, 0), (0, 0), (0, pad)))
    wq_sw = jnp.concatenate([jnp.zeros_like(wq[..., :MLA_NOPE]), _rope_swap(wq[..., MLA_NOPE:])], axis=-1)
    wq_sw = jnp.pad(wq_sw, ((0, 0), (0, 0), (0, pad)))
    hw = MLA_HEADS * HEAD_PAD
    wq2 = jnp.concatenate([wq_pad.reshape(MLA_Q_LORA, hw), wq_sw.reshape(MLA_Q_LORA, hw)], axis=1).astype(BF16)

    wkv = w_ukv.reshape(MLA_KV_LORA, MLA_HEADS, MLA_NOPE + MLA_V)
    wk = jnp.pad(wkv[..., :MLA_NOPE], ((0, 0), (0, 0), (0, HEAD_PAD - MLA_NOPE))).reshape(MLA_KV_LORA, hw)
    wv = jnp.pad(wkv[..., MLA_NOPE:], ((0, 0), (0, 0), (0, HEAD_PAD - MLA_V))).reshape(MLA_KV_LORA, hw)
    wkv2 = jnp.concatenate([wk, wv], axis=1).astype(BF16)

    wmo = w_mla_o.reshape(MLA_HEADS, MLA_V, D_MODEL)
    wmo = jnp.pad(wmo, ((0, 0), (0, HEAD_PAD - MLA_V), (0, 0))).reshape(hw, D_MODEL).astype(BF16)
    return wa, wg, wq2, wkv2, wmo


def _rope_lane_consts():
    lane = jnp.arange(HEAD_PAD)
    half = MLA_ROPE // 2
    inv_freq = 1.0 / (ROPE_THETA ** (jnp.arange(0, MLA_ROPE, 2, dtype=F32) / MLA_ROPE))
    r = lane - MLA_NOPE
    in_rope = (r >= 0) & (r < MLA_ROPE)
    invf = jnp.where(in_rope, inv_freq[jnp.clip(r, 0, MLA_ROPE - 1) % half], 0.0)
    cmask = (lane < MLA_NOPE + MLA_ROPE).astype(F32)
    ssign = jnp.where(in_rope, jnp.where(r < half, -1.0, 1.0), 0.0)
    vone = (jnp.arange(MLA_HEADS * HEAD_PAD) % HEAD_PAD == MLA_V).astype(F32)
    return invf[None].astype(F32), cmask[None], ssign[None].astype(F32), vone[None]


def _s5_block_weights(bb_re, bb_im, c_re, c_im):
    gpc = SSM_CHUNK // SSM_GROUP
    eye = jnp.eye(gpc, dtype=F32)

    def bblock(bb):
        bb = bb.reshape(2, SSM_CHUNKS, gpc, SSM_STATE, SSM_GROUP)
        return jnp.einsum('ab,djapc->djacbp', eye, bb).reshape(2, SSM_CHUNKS, SSM_CHUNK, SSM_CHUNK_STATES)

    def cblock(c):
        c = c.reshape(2, SSM_CHUNKS, gpc, SSM_GROUP, SSM_STATE)
        return jnp.einsum('ab,djacp->djbpac', eye, c).reshape(2, SSM_CHUNKS, SSM_CHUNK_STATES, SSM_CHUNK)

    bm = jnp.concatenate([bblock(bb_re), bblock(bb_im)], axis=-1).astype(BF16)
    cm = jnp.concatenate([cblock(c_re), -cblock(c_im)], axis=-2).astype(BF16)
    return bm, cm


def _front(l, h, mem, positions, g_mix, w_in, b_gate, ssm_lam_re, ssm_lam_im, ssm_log_dt, ssm_b_re,
           ssm_b_im, ssm_c_re, ssm_c_im, ssm_d, w_ssm_glu, g_q_lora, w_uq, g_kv_lora, w_ukv, w_mla_o,
           w_out, g_mem_q, g_mem_kv, w_mem_q, w_mem_kv, w_mem_o, B, L):
    M = mem.shape[1]
    N = B * L
    tl = min(512, L)
    pos2 = positions.reshape(N, 1)
    invf, cmask, ssign, vone = _rope_lane_consts()
    wa, wg, wq2, wkv2, wmo = _prep_mixer_weights(w_in[l], w_uq[l], w_ukv[l], w_mla_o[l])
    u_tm, q, k, v, gates = _in_proj(
        h, pos2, g_mix[l][None], wa, wg, b_gate[l][None], g_q_lora[l][None], wq2,
        g_kv_lora[l][None], wkv2, invf, cmask, ssign, vone, B, L, tl)
    ar, ai, bbr, bbi = _s5_discretise(ssm_lam_re[l], ssm_lam_im[l], ssm_log_dt[l],
                                      ssm_b_re[l], ssm_b_im[l])
    bm, cm = _s5_block_weights(bbr, bbi, ssm_c_re[l], ssm_c_im[l])
    u_rows = u_tm.reshape(L * B, SSM_WIDTH)
    yf, yb = _s5(u_rows, bm, cm, ar.reshape(2, 1, SSM_STATES), ai.reshape(2, 1, SSM_STATES),
                 B, L, min(32, L))
    attn = _attention(q, k, v, B, L, min(512, L), min(1024, L))
    h = _merge(h, yf.reshape(L, B * SSM_WIDTH), yb.reshape(L, B * SSM_WIDTH), u_tm, attn, gates,
               ssm_d[l][None], w_ssm_glu[l].astype(BF16), wmo, w_out[l].astype(BF16), B, L, tl)
    kv_mem = _mem_kv(mem.reshape(B * M, D_MODEL), g_mem_kv[l][None], w_mem_kv[l].astype(BF16), M)
    return _mem_attn(h, kv_mem, g_mem_q[l][None], w_mem_q[l].astype(BF16), w_mem_o[l].astype(BF16),
                     B, L, M, tl)


def kernel(x, mem, positions, g_mix, w_in, b_gate, ssm_lam_re, ssm_lam_im, ssm_log_dt, ssm_b_re, ssm_b_im, ssm_c_re, ssm_c_im, ssm_d, w_ssm_glu, g_q_lora, w_uq, g_kv_lora, w_ukv, w_mla_o, w_out, g_mem_q, g_mem_kv, w_mem_q, w_mem_kv, w_mem_o, g_peer, w_peer_q, peer_sub_keys, peer_u, peer_v, g_final):
    B, L, _ = x.shape
    depth = g_mix.shape[0]
    h = x.reshape(B * L, D_MODEL)
    for l in range(depth):
        h = _front(l, h, mem, positions, g_mix, w_in, b_gate, ssm_lam_re, ssm_lam_im, ssm_log_dt,
                   ssm_b_re, ssm_b_im, ssm_c_re, ssm_c_im, ssm_d, w_ssm_glu, g_q_lora, w_uq, g_kv_lora,
                   w_ukv, w_mla_o, w_out, g_mem_q, g_mem_kv, w_mem_q, w_mem_kv, w_mem_o, B, L)
        p = _peer(h, g_peer[l], w_peer_q[l], peer_sub_keys[l], peer_u[l], peer_v[l])
        if l + 1 < depth:
            h = h + p
    return _final(h, p, g_final[None], min(512, L)).reshape(B, L, D_MODEL)
```

```python
import functools
import math

import jax
import jax.numpy as jnp
from jax import lax
from jax.experimental import pallas as pl
from jax.experimental.pallas import tpu as pltpu

F32 = jnp.float32
BF16 = jnp.bfloat16

D_MODEL = 1024
SSM_WIDTH = 512
SSM_GROUP = 16
SSM_GROUPS = 32
SSM_STATE = 64
MLA_HEADS = 8
MLA_NOPE = 64
MLA_ROPE = 32
MLA_V = 64
MLA_Q_LORA = 384
MLA_KV_LORA = 256
ROPE_THETA = 10000.0
MEM_HEADS = 4
MEM_HEAD_DIM = 256
PEER_HEADS = 8
PEER_N_KEYS = 128
PEER_HALF = 128
PEER_TOPK = 16
NORM_EPS = 1e-6

LANES = 128
SUBLANES = 8
VMEM_LIMIT = 48 * 1024 * 1024

HEAD_PAD = LANES
SSM_CHUNK = LANES
SSM_CHUNKS = SSM_WIDTH // SSM_CHUNK
SSM_CHUNK_STATES = (SSM_CHUNK // SSM_GROUP) * SSM_STATE
SSM_STATES = SSM_GROUPS * SSM_STATE


def _cparams(sem):
    return pltpu.CompilerParams(dimension_semantics=sem, vmem_limit_bytes=VMEM_LIMIT)


def _rms(x, g):
    return x * lax.rsqrt(jnp.mean(x * x, axis=-1, keepdims=True) + NORM_EPS) * g


def _bdot(a, b):
    return jnp.dot(a, b, preferred_element_type=F32)


def _gelu(x):
    return jax.nn.gelu(x, approximate=True)


def _full(shape):
    nd = len(shape)
    return pl.BlockSpec(shape, lambda *_: (0,) * nd)


def _in_proj_kernel(x_ref, pos_ref, gmix_ref, wa_ref, wg_ref, bg_ref, gq_ref, wq_ref,
                    gkv_ref, wkv_ref, invf_ref, cmask_ref, ssign_ref, vone_ref,
                    u_ref, q_ref, k_ref, v_ref, gate_ref):
    xn = _rms(x_ref[...], gmix_ref[...]).astype(BF16)
    proj = _bdot(xn, wa_ref[...])
    u_ref[...] = proj[:, :SSM_WIDTH]
    gate_ref[...] = jax.nn.sigmoid(_bdot(xn, wg_ref[...]) + bg_ref[...]).astype(BF16)

    ang = pos_ref[...].astype(F32) * invf_ref[...]
    cosf = jnp.cos(ang) * cmask_ref[...]
    sinf = jnp.sin(ang) * ssign_ref[...]

    o = SSM_WIDTH
    cq = _rms(proj[:, o:o + MLA_Q_LORA], gq_ref[...]).astype(BF16)
    o += MLA_Q_LORA
    ckv = _rms(proj[:, o:o + MLA_KV_LORA], gkv_ref[...]).astype(BF16)
    o += MLA_KV_LORA
    kr = proj[:, o:o + HEAD_PAD]
    krs = proj[:, o + HEAD_PAD:o + 2 * HEAD_PAD]
    kr_rot = kr * cosf + krs * sinf

    qq = _bdot(cq, wq_ref[...])
    kv = _bdot(ckv, wkv_ref[...])
    scale = (MLA_NOPE + MLA_ROPE) ** -0.5 * math.log2(math.e)
    hw = MLA_HEADS * HEAD_PAD
    for h in range(MLA_HEADS):
        sl = slice(h * HEAD_PAD, (h + 1) * HEAD_PAD)
        sl2 = slice(hw + h * HEAD_PAD, hw + (h + 1) * HEAD_PAD)
        q_ref[:, sl] = ((qq[:, sl] * cosf + qq[:, sl2] * sinf) * scale).astype(BF16)
        k_ref[:, sl] = (kv[:, sl] + kr_rot).astype(BF16)
    v_ref[...] = (kv[:, hw:] + vone_ref[...]).astype(BF16)


def _in_proj(x2, pos2, gmix, wa, wg, bg, gq, wq2, gkv, wkv2, invf, cmask, ssign, vone, B, L, tl):
    N = B * L
    nl = L // tl
    hw = MLA_HEADS * HEAD_PAD
    row = lambda b, i: (b * nl + i, 0)
    consts = [gmix, wa, wg, bg, gq, wq2, gkv, wkv2, invf, cmask, ssign, vone]
    return pl.pallas_call(
        _in_proj_kernel,
        grid=(B, nl),
        in_specs=[pl.BlockSpec((tl, D_MODEL), row), pl.BlockSpec((tl, 1), row)]
        + [_full(c.shape) for c in consts],
        out_specs=[pl.BlockSpec((tl, SSM_WIDTH), lambda b, i: (i, b)),
                   pl.BlockSpec((tl, hw), row), pl.BlockSpec((tl, hw), row),
                   pl.BlockSpec((tl, hw), row), pl.BlockSpec((tl, 2 * D_MODEL), row)],
        out_shape=[jax.ShapeDtypeStruct((L, B * SSM_WIDTH), F32),
                   jax.ShapeDtypeStruct((N, hw), BF16), jax.ShapeDtypeStruct((N, hw), BF16),
                   jax.ShapeDtypeStruct((N, hw), BF16), jax.ShapeDtypeStruct((N, 2 * D_MODEL), BF16)],
        compiler_params=_cparams(("arbitrary", "arbitrary")),
        name="in_proj",
    )(x2, pos2, *consts)


def _s5_disc_kernel(lr_ref, li_ref, ldt_ref, bre_ref, bim_ref, ar_ref, ai_ref, bbr_ref, bbi_ref):
    lr = lr_ref[...]
    li = li_ref[...]
    dt = jnp.exp(ldt_ref[...])
    mag = jnp.exp(lr * dt)
    ar = mag * jnp.cos(li * dt)
    ai = mag * jnp.sin(li * dt)
    nr = ar - 1.0
    den = lr * lr + li * li
    fr = (nr * lr + ai * li) / den
    fi = (ai * lr - nr * li) / den
    ar_ref[...] = ar
    ai_ref[...] = ai
    bre = bre_ref[...]
    bim = bim_ref[...]
    bbr_ref[...] = fr * bre - fi * bim
    bbi_ref[...] = fr * bim + fi * bre


def _s5_discretise(lam_re, lam_im, log_dt, b_re, b_im):
    n = lam_re.size
    ldt = jnp.broadcast_to(log_dt[..., None], lam_re.shape).reshape(n, 1)
    sds = jax.ShapeDtypeStruct
    ar, ai, bbr, bbi = pl.pallas_call(
        _s5_disc_kernel,
        out_shape=[sds((n, 1), F32), sds((n, 1), F32), sds((n, SSM_GROUP), F32), sds((n, SSM_GROUP), F32)],
        name="s5_discretise",
    )(lam_re.reshape(n, 1), lam_im.reshape(n, 1), ldt,
      b_re.reshape(n, SSM_GROUP), b_im.reshape(n, SSM_GROUP))
    return (ar.reshape(lam_re.shape), ai.reshape(lam_re.shape),
            bbr.reshape(b_re.shape), bbi.reshape(b_re.shape))


def _s5_kernel(uf_ref, ub_ref, bm_ref, cm_ref, ar_ref, ai_ref, yf_ref, yb_ref,
               st_ref, sf_ref, sb_ref, *, tt, nb):
    S = SSM_STATES
    CS = SSM_CHUNK_STATES

    @pl.when(pl.program_id(0) == 0)
    def _():
        st_ref[...] = jnp.zeros_like(st_ref)

    for d, (u_ref, s_ref) in enumerate(((uf_ref, sf_ref), (ub_ref, sb_ref))):
        ub = u_ref[...].astype(BF16)
        for j in range(SSM_CHUNKS):
            bu = _bdot(ub[:, j * SSM_CHUNK:(j + 1) * SSM_CHUNK], bm_ref[d, j])
            s_ref[:, j * CS:(j + 1) * CS] = bu[:, :CS]
            s_ref[:, S + j * CS:S + (j + 1) * CS] = bu[:, CS:]

    for c in range(S // CS):
        lre = slice(c * CS, (c + 1) * CS)
        lim = slice(S + c * CS, S + (c + 1) * CS)
        arf = jnp.broadcast_to(ar_ref[0, :, lre], (nb, CS))
        aif = jnp.broadcast_to(ai_ref[0, :, lre], (nb, CS))
        arb = jnp.broadcast_to(ar_ref[1, :, lre], (nb, CS))
        aib = jnp.broadcast_to(ai_ref[1, :, lre], (nb, CS))

        def body(t, carry):
            fr, fi, br, bi = carry
            rf = pl.multiple_of(t * nb, nb)
            rb = pl.multiple_of((tt - 1 - t) * nb, nb)
            nfr = arf * fr - aif * fi + sf_ref[pl.ds(rf, nb), lre]
            nfi = arf * fi + aif * fr + sf_ref[pl.ds(rf, nb), lim]
            sf_ref[pl.ds(rf, nb), lre] = nfr
            sf_ref[pl.ds(rf, nb), lim] = nfi
            nbr = arb * br - aib * bi + sb_ref[pl.ds(rb, nb), lre]
            nbi = arb * bi + aib * br + sb_ref[pl.ds(rb, nb), lim]
            sb_ref[pl.ds(rb, nb), lre] = nbr
            sb_ref[pl.ds(rb, nb), lim] = nbi
            return nfr, nfi, nbr, nbi

        init = (st_ref[0, :, lre], st_ref[0, :, lim], st_ref[1, :, lre], st_ref[1, :, lim])
        fr, fi, br, bi = lax.fori_loop(0, tt, body, init)
        st_ref[0, :, lre] = fr
        st_ref[0, :, lim] = fi
        st_ref[1, :, lre] = br
        st_ref[1, :, lim] = bi

    for d, (s_ref, y_ref) in enumerate(((sf_ref, yf_ref), (sb_ref, yb_ref))):
        for j in range(SSM_CHUNKS):
            sre = s_ref[:, j * CS:(j + 1) * CS].astype(BF16)
            sim = s_ref[:, S + j * CS:S + (j + 1) * CS].astype(BF16)
            y_ref[:, j * SSM_CHUNK:(j + 1) * SSM_CHUNK] = (
                _bdot(sre, cm_ref[d, j, :CS, :]) + _bdot(sim, cm_ref[d, j, CS:, :]))


def _s5(u_tm, bm, cm, ar, ai, B, L, tt):
    nblk = L // tt
    R = tt * B
    blk = lambda idx: pl.BlockSpec((R, SSM_WIDTH), idx)
    fwd = lambda i: (i, 0)
    bwd = lambda i: (nblk - 1 - i, 0)
    return pl.pallas_call(
        functools.partial(_s5_kernel, tt=tt, nb=B),
        grid=(nblk,),
        in_specs=[blk(fwd), blk(bwd), _full(bm.shape), _full(cm.shape), _full(ar.shape), _full(ai.shape)],
        out_specs=[blk(fwd), blk(bwd)],
        out_shape=[jax.ShapeDtypeStruct((L * B, SSM_WIDTH), F32)] * 2,
        scratch_shapes=[pltpu.VMEM((2, B, 2 * SSM_STATES), F32),
                        pltpu.VMEM((R, 2 * SSM_STATES), F32),
                        pltpu.VMEM((R, 2 * SSM_STATES), F32)],
        compiler_params=_cparams(("arbitrary",)),
        name="s5_scan",
    )(u_tm, u_tm, bm, cm, ar, ai)


def _attn_kernel(q_ref, k_ref, v_ref, o_ref, *, ck):
    q = q_ref[...]
    tq = q.shape[0]
    m = jnp.full((tq, 1), _NEG_INF, F32)
    acc = jnp.zeros((tq, HEAD_PAD), F32)
    for j in range(k_ref.shape[0] // ck):
        s = lax.dot_general(q, k_ref[j * ck:(j + 1) * ck, :], (((1,), (1,)), ((), ())),
                            preferred_element_type=F32)
        m_new = jnp.maximum(m, jnp.max(s, axis=-1, keepdims=True))
        p = jnp.exp2(s - m_new).astype(BF16)
        acc = acc * jnp.exp2(m - m_new) + _bdot(p, v_ref[j * ck:(j + 1) * ck, :])
        m = m_new
    o_ref[...] = (acc / acc[:, MLA_V:MLA_V + 1]).astype(BF16)


def _attention(q, k, v, B, L, tq, ck):
    nq = L // tq
    return pl.pallas_call(
        functools.partial(_attn_kernel, ck=ck),
        grid=(B, MLA_HEADS, nq),
        in_specs=[pl.BlockSpec((tq, HEAD_PAD), lambda b, h, i: (b * nq + i, h)),
                  pl.BlockSpec((L, HEAD_PAD), lambda b, h, i: (b, h)),
                  pl.BlockSpec((L, HEAD_PAD), lambda b, h, i: (b, h))],
        out_specs=pl.BlockSpec((tq, HEAD_PAD), lambda b, h, i: (b * nq + i, h)),
        out_shape=jax.ShapeDtypeStruct(q.shape, BF16),
        compiler_params=_cparams(("arbitrary", "arbitrary", "arbitrary")),
        name="mla_attention",
    )(q, k, v)


def _merge_kernel(x_ref, yf_ref, yb_ref, u_ref, at_ref, gate_ref, d_ref, wglu_ref, wmo_ref, wout_ref,
                  h_ref):
    y = yf_ref[...] + yb_ref[...] + d_ref[...] * u_ref[...]
    y = _gelu(y).astype(BF16)
    glu = _bdot(y, wglu_ref[...])
    branch_ssm = glu[:, :D_MODEL] * jax.nn.sigmoid(glu[:, D_MODEL:])
    branch_mla = _bdot(at_ref[...], wmo_ref[...])
    g = gate_ref[...].astype(F32)
    merged = (g[:, :D_MODEL] * branch_ssm + g[:, D_MODEL:] * branch_mla).astype(BF16)
    h_ref[...] = x_ref[...] + _bdot(merged, wout_ref[...])


def _merge(x2, yf_tm, yb_tm, u_tm, attn, gates, d_skip, wglu, wmo, wout, B, L, tl):
    nl = L // tl
    row = lambda b, i: (b * nl + i, 0)
    tm = pl.BlockSpec((tl, SSM_WIDTH), lambda b, i: (i, b))
    consts = [d_skip, wglu, wmo, wout]
    return pl.pallas_call(
        _merge_kernel,
        grid=(B, nl),
        in_specs=[pl.BlockSpec((tl, D_MODEL), row), tm, tm, tm,
                  pl.BlockSpec((tl, attn.shape[1]), row), pl.BlockSpec((tl, 2 * D_MODEL), row)]
        + [_full(c.shape) for c in consts],
        out_specs=pl.BlockSpec((tl, D_MODEL), row),
        out_shape=jax.ShapeDtypeStruct(x2.shape, F32),
        compiler_params=_cparams(("arbitrary", "arbitrary")),
        name="merge_out",
    )(x2, yf_tm, yb_tm, u_tm, attn, gates, *consts)


def _mem_kv_kernel(mem_ref, g_ref, w_ref, kv_ref):
    mn = _rms(mem_ref[...], g_ref[...]).astype(BF16)
    kv_ref[...] = _bdot(mn, w_ref[...]).astype(BF16)


def _mem_kv(mem2, g, w, tm):
    n = mem2.shape[0]
    return pl.pallas_call(
        _mem_kv_kernel,
        grid=(n // tm,),
        in_specs=[pl.BlockSpec((tm, D_MODEL), lambda i: (i, 0)), _full(g.shape), _full(w.shape)],
        out_specs=pl.BlockSpec((tm, 2 * D_MODEL), lambda i: (i, 0)),
        out_shape=jax.ShapeDtypeStruct((n, 2 * D_MODEL), BF16),
        compiler_params=_cparams(("arbitrary",)),
        name="mem_kv",
    )(mem2, g, w)


def _mem_attn_kernel(h_ref, kv_ref, g_ref, wq_ref, wo_ref, o_ref):
    h = h_ref[...]
    q = _bdot(_rms(h, g_ref[...]).astype(BF16), wq_ref[...])
    scale = MEM_HEAD_DIM ** -0.5
    outs = []
    for hd in range(MEM_HEADS):
        sl = slice(hd * MEM_HEAD_DIM, (hd + 1) * MEM_HEAD_DIM)
        kh = kv_ref[:, sl]
        vh = kv_ref[:, D_MODEL + hd * MEM_HEAD_DIM:D_MODEL + (hd + 1) * MEM_HEAD_DIM]
        s = lax.dot_general(q[:, sl].astype(BF16), kh, (((1,), (1,)), ((), ())),
                            preferred_element_type=F32) * scale
        p = jnp.exp(s - jnp.max(s, axis=-1, keepdims=True))
        p = p / jnp.sum(p, axis=-1, keepdims=True)
        outs.append(_bdot(p.astype(BF16), vh).astype(BF16))
    o = jnp.concatenate(outs, axis=-1)
    o_ref[...] = h + _bdot(o, wo_ref[...])


def _mem_attn(h2, kv, g, wq, wo, B, L, M, tl):
    nl = L // tl
    row = lambda b, i: (b * nl + i, 0)
    consts = [g, wq, wo]
    return pl.pallas_call(
        _mem_attn_kernel,
        grid=(B, nl),
        in_specs=[pl.BlockSpec((tl, D_MODEL), row), pl.BlockSpec((M, 2 * D_MODEL), lambda b, i: (b, 0))]
        + [_full(c.shape) for c in consts],
        out_specs=pl.BlockSpec((tl, D_MODEL), row),
        out_shape=jax.ShapeDtypeStruct(h2.shape, F32),
        compiler_params=_cparams(("arbitrary", "arbitrary")),
        name="mem_attention",
    )(h2, kv, *consts)


_NEG_INF = float("-inf")
_BIG = 3.0e38
PEER_EXPERTS = PEER_N_KEYS * PEER_N_KEYS


def _topk_rows(problems, k):
    ss = [s for s, _ in problems]
    vals = [[] for _ in problems]
    pays = [[] for _ in problems]
    for _ in range(k):
        for i, (_, payload) in enumerate(problems):
            m = jnp.max(ss[i], axis=0, keepdims=True)
            cand = jnp.where(ss[i] == m, payload, _BIG)
            p = jnp.min(cand, axis=0, keepdims=True)
            vals[i].append(m)
            pays[i].append(p)
            ss[i] = jnp.where(cand == p, _NEG_INF, ss[i])
    return [(jnp.concatenate(v, axis=0), jnp.concatenate(p, axis=0)) for v, p in zip(vals, pays)]


def _peer_candidates(s1, i1, s2, i2, T):
    K = PEER_TOPK
    E = float(PEER_EXPERTS)
    r8 = lax.broadcasted_iota(jnp.int32, (8, T), 0)
    r8f = r8.astype(F32)
    r16f = lax.broadcasted_iota(jnp.int32, (K, T), 0).astype(F32)
    cs = [s1[0:1] + s2]
    pay = [r16f * E + (i1[0:1] * PEER_N_KEYS + i2)]
    for r1 in range(1, 8):
        ok = r8 < K // (r1 + 1)
        cs.append(jnp.where(ok, s1[r1:r1 + 1] + s2[0:8], _NEG_INF))
        pay.append((r8f + r1 * K) * E + (i1[r1:r1 + 1] * PEER_N_KEYS + i2[0:8]))
    cs.append(s1[8:K] + s2[0:1])
    pay.append((r8f + 8.0) * (K * E) + (i1[8:K] * PEER_N_KEYS + i2[0:1]))
    return jnp.concatenate(cs, axis=0), jnp.concatenate(pay, axis=0)


def _peer_topk_kernel(h_ref, g_ref, wq_ref, keys_ref, hn_ref, idx_ref, gate_ref, q_scr, *, T, hp):
    hd = pl.program_id(1)

    @pl.when(hd == 0)
    def _():
        hn = _rms(h_ref[...], g_ref[...])
        hn_ref[...] = hn
        q = _bdot(hn.astype(BF16), wq_ref[...]).astype(BF16)
        for j in range(2 * PEER_HEADS):
            q_scr[j] = q[:, j * PEER_HALF:(j + 1) * PEER_HALF]

    n_iota = lax.broadcasted_iota(jnp.int32, (PEER_N_KEYS, T), 0).astype(F32)
    K = PEER_TOPK
    problems = []
    for j in range(2 * hp):
        hs = 2 * hp * hd + j
        s = lax.dot_general(keys_ref[hs], q_scr[hs], (((1,), (1,)), ((), ())),
                            preferred_element_type=F32)
        problems.append((s, n_iota))
    tops = _topk_rows(problems, K)
    cands = [_peer_candidates(*tops[2 * i], *tops[2 * i + 1], T) for i in range(hp)]
    for i, (top_s, top_pay) in enumerate(_topk_rows(cands, K)):
        expert = top_pay - jnp.floor(top_pay * (1.0 / PEER_EXPERTS)) * PEER_EXPERTS
        e = jnp.exp(top_s - top_s[0:1])
        gate_ref[i * K:(i + 1) * K, :] = e / jnp.sum(e, axis=0, keepdims=True)
        idx_ref[i * K:(i + 1) * K, :] = expert.astype(jnp.int32) * TABLE_ROWS


def _peer_topk(h2, g, wq, keys, T, hp):
    N = h2.shape[0]
    K = PEER_TOPK * hp
    return pl.pallas_call(
        functools.partial(_peer_topk_kernel, T=T, hp=hp),
        grid=(N // T, PEER_HEADS // hp),
        in_specs=[pl.BlockSpec((T, D_MODEL), lambda i, h: (i, 0)), _full(g.shape), _full(wq.shape),
                  _full(keys.shape)],
        out_specs=[pl.BlockSpec((T, D_MODEL), lambda i, h: (i, 0)),
                   pl.BlockSpec((K, T), lambda i, h: (h, i)),
                   pl.BlockSpec((K, T), lambda i, h: (h, i))],
        out_shape=[jax.ShapeDtypeStruct((N, D_MODEL), F32),
                   jax.ShapeDtypeStruct((PEER_SLOTS, N), jnp.int32),
                   jax.ShapeDtypeStruct((PEER_SLOTS, N), F32)],
        scratch_shapes=[pltpu.VMEM((2 * PEER_HEADS, T, PEER_HALF), BF16)],
        compiler_params=_cparams(("arbitrary", "arbitrary")),
        name="peer_topk",
    )(h2, g, wq, keys)


PEER_SLOTS = PEER_HEADS * PEER_TOPK
TABLE_ROWS = D_MODEL // (2 * LANES)
PEER_VMEM_LIMIT = 56 * 1024 * 1024


GATHER_ROWS = PEER_SLOTS * TABLE_ROWS
GATHER_BF16_ROWS = 2 * GATHER_ROWS


def _chunk_mask():
    shape = (SUBLANES, GATHER_BF16_ROWS)
    c = lax.broadcasted_iota(jnp.int32, shape, 0)
    j = lax.broadcasted_iota(jnp.int32, shape, 1) & (2 * TABLE_ROWS - 1)
    return c == (j >> 1) + TABLE_ROWS * (j & 1)


def _split_bf16(x):
    hi = x.astype(BF16)
    return hi, (x - hi.astype(F32)).astype(BF16)


GROUP = SUBLANES
HALF_GROUPS = 2
STEP_TOKENS = 2 * HALF_GROUPS * GROUP


def _gather_group(idx_s, r0, tab_ref, g_buf):
    for j in range(GROUP):
        for k in range(PEER_SLOTS):
            row = pl.multiple_of(idx_s[r0 + j, k], TABLE_ROWS)
            g_buf[j, k * TABLE_ROWS:(k + 1) * TABLE_ROWS, :] = tab_ref[pl.ds(row, TABLE_ROWS), :]


def _pipelined_groups(idx_hbm, tab_ref, idx_bufs, idx_first, sems, g_bufs, consume):
    i = pl.program_id(0)
    nsteps = pl.num_programs(0)
    rows = HALF_GROUPS * GROUP

    def idx_copy(half, c):
        src = idx_hbm.at[pl.ds(pl.multiple_of((half * HALF_GROUPS + 1) * GROUP, GROUP), rows)]
        return pltpu.make_async_copy(src, idx_bufs[c], sems.at[c])

    @pl.when(i == 0)
    def _():
        first = pltpu.make_async_copy(idx_hbm.at[pl.ds(0, GROUP)], idx_first, sems.at[2])
        first.start()
        idx_copy(0, 0).start()
        first.wait()
        _gather_group(idx_first, 0, tab_ref, g_bufs[0])

    for c in range(2):
        half = 2 * i + c
        idx_copy(half, c).wait()
        if c == 0:
            idx_copy(half + 1, 1).start()
        else:
            @pl.when(i + 1 < nsteps)
            def _():
                idx_copy(half + 1, 0).start()
        for q in range(HALF_GROUPS):
            _gather_group(idx_bufs[c], q * GROUP, tab_ref, g_bufs[(q + 1) % 2])
            consume(g_bufs[q % 2], (c * HALF_GROUPS + q) * GROUP)


def _peer_u_kernel(idx_hbm, x_ref, gate_ref, tab_ref, e8_ref, w_ref, g_a, g_b, s0, s1, sf, sems):
    mask = _chunk_mask()
    rows = lax.broadcasted_iota(jnp.int32, (SUBLANES, GATHER_BF16_ROWS), 0)
    nt = (((1,), (1,)), ((), ()))

    def consume(g_buf, t0):
        dall = jnp.zeros((SUBLANES, GATHER_BF16_ROWS), F32)
        for j in range(GROUP):
            gb = pltpu.bitcast(g_buf[j], BF16)
            r = lax.dot_general(x_ref[t0 + j].astype(BF16), gb, nt,
                                preferred_element_type=F32)
            d = jnp.sum(jnp.where(mask, r, 0.0), axis=0, keepdims=True)
            dall = jnp.where(rows == j, d, dall)
        hi, lo = _split_bf16(dall)
        act = _bdot(hi, e8_ref[...]) + _bdot(lo, e8_ref[...])
        w_ref[t0:t0 + GROUP, :] = gate_ref[t0:t0 + GROUP, :] * _gelu(act)

    _pipelined_groups(idx_hbm, tab_ref, (s0, s1), sf, sems, (g_a, g_b), consume)


def _peer_v_kernel(idx_hbm, w_ref, tab_ref, e8t_ref, o_ref, g_a, g_b, s0, s1, sf, sems):
    mask = _chunk_mask()
    shape = (SUBLANES, GATHER_BF16_ROWS)

    def consume(g_buf, t0):
        hi, lo = _split_bf16(w_ref[t0:t0 + GROUP, :])
        rep_hi = _bdot(hi, e8t_ref[...])
        rep_lo = _bdot(lo, e8t_ref[...])
        for j in range(GROUP):
            gb = pltpu.bitcast(g_buf[j], BF16)
            whi = jnp.where(mask, jnp.broadcast_to(rep_hi[j:j + 1], shape), 0.0).astype(BF16)
            wlo = jnp.where(mask, jnp.broadcast_to(rep_lo[j:j + 1], shape), 0.0).astype(BF16)
            o_ref[t0 + j] = _bdot(whi, gb) + _bdot(wlo, gb)

    _pipelined_groups(idx_hbm, tab_ref, (s0, s1), sf, sems, (g_a, g_b), consume)


def _peer_cparams():
    return pltpu.CompilerParams(dimension_semantics=("arbitrary",), vmem_limit_bytes=PEER_VMEM_LIMIT)


def _peer_scratch():
    return ([pltpu.VMEM((GROUP, GATHER_ROWS, LANES), jnp.int32)] * 2
            + [pltpu.SMEM((HALF_GROUPS * GROUP, PEER_SLOTS), jnp.int32)] * 2
            + [pltpu.SMEM((GROUP, PEER_SLOTS), jnp.int32), pltpu.SemaphoreType.DMA((3,))])


def _slot_expand():
    return jnp.repeat(jnp.eye(PEER_SLOTS, dtype=BF16), 2 * TABLE_ROWS, axis=0)


def _peer_u(idx_pad, x3, gates, tab):
    N = x3.shape[0]
    tb = STEP_TOKENS
    e8 = _slot_expand()
    tile = pl.BlockSpec((tb, PEER_SLOTS), lambda i: (i, 0))
    return pl.pallas_call(
        _peer_u_kernel,
        grid=(N // tb,),
        in_specs=[pl.BlockSpec(memory_space=pl.ANY),
                  pl.BlockSpec((tb, SUBLANES, LANES), lambda i: (i, 0, 0)), tile,
                  _full(tab.shape), _full(e8.shape)],
        out_specs=tile,
        out_shape=jax.ShapeDtypeStruct((N, PEER_SLOTS), F32),
        scratch_shapes=_peer_scratch(),
        compiler_params=_peer_cparams(),
        name="peer_u_gather",
    )(idx_pad, x3, gates, tab, e8)


def _peer_v(idx_pad, w, tab):
    N = w.shape[0]
    tb = STEP_TOKENS
    e8t = _slot_expand().T
    return pl.pallas_call(
        _peer_v_kernel,
        grid=(N // tb,),
        in_specs=[pl.BlockSpec(memory_space=pl.ANY), pl.BlockSpec((tb, PEER_SLOTS), lambda i: (i, 0)),
                  _full(tab.shape), _full(e8t.shape)],
        out_specs=pl.BlockSpec((tb, SUBLANES, LANES), lambda i: (i, 0, 0)),
        out_shape=jax.ShapeDtypeStruct((N, SUBLANES, LANES), F32),
        scratch_shapes=_peer_scratch(),
        compiler_params=_peer_cparams(),
        name="peer_v_gather",
    )(idx_pad, w, tab, e8t)


def _pack_table(t):
    E = t.shape[0]
    b = lax.bitcast_convert_type(t.astype(BF16), jnp.uint16).astype(jnp.uint32)
    b = b.reshape(E, 2, TABLE_ROWS, LANES)
    word = b[:, 0] | (b[:, 1] << 16)
    return lax.bitcast_convert_type(word, jnp.int32).reshape(E * TABLE_ROWS, LANES)


def _final_kernel(h_ref, p_ref, g_ref, o_ref):
    o_ref[...] = _rms(h_ref[...] + p_ref[...], g_ref[...])


def _final(h2, peer_out, g, tl):
    N = h2.shape[0]
    row = pl.BlockSpec((tl, D_MODEL), lambda i: (i, 0))
    return pl.pallas_call(
        _final_kernel,
        grid=(N // tl,),
        in_specs=[row, row, _full(g.shape)],
        out_specs=row,
        out_shape=jax.ShapeDtypeStruct(h2.shape, F32),
        compiler_params=_cparams(("arbitrary",)),
        name="final_norm",
    )(h2, peer_out, g)


def _peer(h2, g_peer, w_peer_q, sub_keys, u_emb, v_emb):
    N = h2.shape[0]
    keys = sub_keys.reshape(2 * PEER_HEADS, PEER_N_KEYS, PEER_HALF).astype(BF16)
    hn, idx_t, gate_t = _peer_topk(h2, g_peer[None], w_peer_q.astype(BF16), keys, min(256, N), 4)
    idx_pad = jnp.pad(idx_t.T, ((0, GROUP), (0, 0)))
    gates = gate_t.T
    w = _peer_u(idx_pad, hn.reshape(N, SUBLANES, LANES), gates, _pack_table(u_emb))
    out = _peer_v(idx_pad, w, _pack_table(v_emb))
    return out.reshape(N, D_MODEL)


def _rope_swap(w):
    half = MLA_ROPE // 2
    return jnp.concatenate([w[..., half:], w[..., :half]], axis=-1)


def _pad_cols(w, width):
    return jnp.pad(w, ((0, 0), (0, width - w.shape[1])))


def _prep_mixer_weights(w_in, w_uq, w_ukv, w_mla_o):
    o0, o1, o2, o3 = SSM_WIDTH, SSM_WIDTH + MLA_Q_LORA, SSM_WIDTH + MLA_Q_LORA + MLA_KV_LORA, None
    o3 = o2 + MLA_ROPE
    w_kr = w_in[:, o2:o3]
    zl = jnp.zeros((D_MODEL, MLA_NOPE), F32)
    kr_pad = _pad_cols(jnp.concatenate([zl, w_kr], axis=1), HEAD_PAD)
    krs_pad = _pad_cols(jnp.concatenate([zl, _rope_swap(w_kr)], axis=1), HEAD_PAD)
    wa = jnp.concatenate([w_in[:, :o2], kr_pad, krs_pad], axis=1).astype(BF16)
    wg = w_in[:, o3:].astype(BF16)

    wq = w_uq.reshape(MLA_Q_LORA, MLA_HEADS, MLA_NOPE + MLA_ROPE)
    pad = HEAD_PAD - MLA_NOPE - MLA_ROPE
    wq_pad = jnp.pad(wq, ((0, 0), (0, 0), (0, pad)))
    wq_sw = jnp.concatenate([jnp.zeros_like(wq[..., :MLA_NOPE]), _rope_swap(wq[..., MLA_NOPE:])], axis=-1)
    wq_sw = jnp.pad(wq_sw, ((0, 0), (0, 0), (0, pad)))
    hw = MLA_HEADS * HEAD_PAD
    wq2 = jnp.concatenate([wq_pad.reshape(MLA_Q_LORA, hw), wq_sw.reshape(MLA_Q_LORA, hw)], axis=1).astype(BF16)

    wkv = w_ukv.reshape(MLA_KV_LORA, MLA_HEADS, MLA_NOPE + MLA_V)
    wk = jnp.pad(wkv[..., :MLA_NOPE], ((0, 0), (0, 0), (0, HEAD_PAD - MLA_NOPE))).reshape(MLA_KV_LORA, hw)
    wv = jnp.pad(wkv[..., MLA_NOPE:], ((0, 0), (0, 0), (0, HEAD_PAD - MLA_V))).reshape(MLA_KV_LORA, hw)
    wkv2 = jnp.concatenate([wk, wv], axis=1).astype(BF16)

    wmo = w_mla_o.reshape(MLA_HEADS, MLA_V, D_MODEL)
    wmo = jnp.pad(wmo, ((0, 0), (0, HEAD_PAD - MLA_V), (0, 0))).reshape(hw, D_MODEL).astype(BF16)
    return wa, wg, wq2, wkv2, wmo


def _rope_lane_consts():
    lane = jnp.arange(HEAD_PAD)
    half = MLA_ROPE // 2
    inv_freq = 1.0 / (ROPE_THETA ** (jnp.arange(0, MLA_ROPE, 2, dtype=F32) / MLA_ROPE))
    r = lane - MLA_NOPE
    in_rope = (r >= 0) & (r < MLA_ROPE)
    invf = jnp.where(in_rope, inv_freq[jnp.clip(r, 0, MLA_ROPE - 1) % half], 0.0)
    cmask = (lane < MLA_NOPE + MLA_ROPE).astype(F32)
    ssign = jnp.where(in_rope, jnp.where(r < half, -1.0, 1.0), 0.0)
    vone = (jnp.arange(MLA_HEADS * HEAD_PAD) % HEAD_PAD == MLA_V).astype(F32)
    return invf[None].astype(F32), cmask[None], ssign[None].astype(F32), vone[None]


def _s5_block_weights(bb_re, bb_im, c_re, c_im):
    gpc = SSM_CHUNK // SSM_GROUP
    eye = jnp.eye(gpc, dtype=F32)

    def bblock(bb):
        bb = bb.reshape(2, SSM_CHUNKS, gpc, SSM_STATE, SSM_GROUP)
        return jnp.einsum('ab,djapc->djacbp', eye, bb).reshape(2, SSM_CHUNKS, SSM_CHUNK, SSM_CHUNK_STATES)

    def cblock(c):
        c = c.reshape(2, SSM_CHUNKS, gpc, SSM_GROUP, SSM_STATE)
        return jnp.einsum('ab,djacp->djbpac', eye, c).reshape(2, SSM_CHUNKS, SSM_CHUNK_STATES, SSM_CHUNK)

    bm = jnp.concatenate([bblock(bb_re), bblock(bb_im)], axis=-1).astype(BF16)
    cm = jnp.concatenate([cblock(c_re), -cblock(c_im)], axis=-2).astype(BF16)
    return bm, cm


def _front(l, h, mem, positions, g_mix, w_in, b_gate, ssm_lam_re, ssm_lam_im, ssm_log_dt, ssm_b_re,
           ssm_b_im, ssm_c_re, ssm_c_im, ssm_d, w_ssm_glu, g_q_lora, w_uq, g_kv_lora, w_ukv, w_mla_o,
           w_out, g_mem_q, g_mem_kv, w_mem_q, w_mem_kv, w_mem_o, B, L):
    M = mem.shape[1]
    N = B * L
    tl = min(512, L)
    pos2 = positions.reshape(N, 1)
    invf, cmask, ssign, vone = _rope_lane_consts()
    wa, wg, wq2, wkv2, wmo = _prep_mixer_weights(w_in[l], w_uq[l], w_ukv[l], w_mla_o[l])
    u_tm, q, k, v, gates = _in_proj(
        h, pos2, g_mix[l][None], wa, wg, b_gate[l][None], g_q_lora[l][None], wq2,
        g_kv_lora[l][None], wkv2, invf, cmask, ssign, vone, B, L, tl)
    ar, ai, bbr, bbi = _s5_discretise(ssm_lam_re[l], ssm_lam_im[l], ssm_log_dt[l],
                                      ssm_b_re[l], ssm_b_im[l])
    bm, cm = _s5_block_weights(bbr, bbi, ssm_c_re[l], ssm_c_im[l])
    u_rows = u_tm.reshape(L * B, SSM_WIDTH)
    yf, yb = _s5(u_rows, bm, cm, ar.reshape(2, 1, SSM_STATES), ai.reshape(2, 1, SSM_STATES),
                 B, L, min(32, L))
    attn = _attention(q, k, v, B, L, min(512, L), min(1024, L))
    h = _merge(h, yf.reshape(L, B * SSM_WIDTH), yb.reshape(L, B * SSM_WIDTH), u_tm, attn, gates,
               ssm_d[l][None], w_ssm_glu[l].astype(BF16), wmo, w_out[l].astype(BF16), B, L, tl)
    kv_mem = _mem_kv(mem.reshape(B * M, D_MODEL), g_mem_kv[l][None], w_mem_kv[l].astype(BF16), M)
    return _mem_attn(h, kv_mem, g_mem_q[l][None], w_mem_q[l].astype(BF16), w_mem_o[l].astype(BF16),
                     B, L, M, tl)


def kernel(x, mem, positions, g_mix, w_in, b_gate, ssm_lam_re, ssm_lam_im, ssm_log_dt, ssm_b_re, ssm_b_im, ssm_c_re, ssm_c_im, ssm_d, w_ssm_glu, g_q_lora, w_uq, g_kv_lora, w_ukv, w_mla_o, w_out, g_mem_q, g_mem_kv, w_mem_q, w_mem_kv, w_mem_o, g_peer, w_peer_q, peer_sub_keys, peer_u, peer_v, g_final):
    B, L, _ = x.shape
    depth = g_mix.shape[0]
    h = x.reshape(B * L, D_MODEL)
    for l in range(depth):
        h = _front(l, h, mem, positions, g_mix, w_in, b_gate, ssm_lam_re, ssm_lam_im, ssm_log_dt,
                   ssm_b_re, ssm_b_im, ssm_c_re, ssm_c_im, ssm_d, w_ssm_glu, g_q_lora, w_uq, g_kv_lora,
                   w_ukv, w_mla_o, w_out, g_mem_q, g_mem_kv, w_mem_q, w_mem_kv, w_mem_o, B, L)
        p = _peer(h, g_peer[l], w_peer_q[l], peer_sub_keys[l], peer_u[l], peer_v[l])
        if l + 1 < depth:
            h = h + p
    return _final(h, p, g_final[None], min(512, L)).reshape(B, L, D_MODEL)
```

```python
import functools
import math

import jax
import jax.numpy as jnp
from jax import lax
from jax.experimental import pallas as pl
from jax.experimental.pallas import tpu as pltpu

F32 = jnp.float32
BF16 = jnp.bfloat16

D_MODEL = 1024
SSM_WIDTH = 512
SSM_GROUP = 16
SSM_GROUPS = 32
SSM_STATE = 64
MLA_HEADS = 8
MLA_NOPE = 64
MLA_ROPE = 32
MLA_V = 64
MLA_Q_LORA = 384
MLA_KV_LORA = 256
ROPE_THETA = 10000.0
MEM_HEADS = 4
MEM_HEAD_DIM = 256
PEER_HEADS = 8
PEER_N_KEYS = 128
PEER_HALF = 128
PEER_TOPK = 16
NORM_EPS = 1e-6

LANES = 128
SUBLANES = 8
VMEM_LIMIT = 48 * 1024 * 1024

HEAD_PAD = LANES
SSM_CHUNK = LANES
SSM_CHUNKS = SSM_WIDTH // SSM_CHUNK
SSM_CHUNK_STATES = (SSM_CHUNK // SSM_GROUP) * SSM_STATE
SSM_STATES = SSM_GROUPS * SSM_STATE


def _cparams(sem):
    return pltpu.CompilerParams(dimension_semantics=sem, vmem_limit_bytes=VMEM_LIMIT)


def _rms(x, g):
    return x * lax.rsqrt(jnp.mean(x * x, axis=-1, keepdims=True) + NORM_EPS) * g


def _bdot(a, b):
    return jnp.dot(a, b, preferred_element_type=F32)


def _gelu(x):
    return jax.nn.gelu(x, approximate=True)


def _full(shape):
    nd = len(shape)
    return pl.BlockSpec(shape, lambda *_: (0,) * nd)


def _in_proj_kernel(x_ref, pos_ref, gmix_ref, wa_ref, wg_ref, bg_ref, gq_ref, wq_ref,
                    gkv_ref, wkv_ref, invf_ref, cmask_ref, ssign_ref, vone_ref,
                    u_ref, q_ref, k_ref, v_ref, gate_ref):
    xn = _rms(x_ref[...], gmix_ref[...]).astype(BF16)
    proj = _bdot(xn, wa_ref[...])
    u_ref[...] = proj[:, :SSM_WIDTH]
    gate_ref[...] = jax.nn.sigmoid(_bdot(xn, wg_ref[...]) + bg_ref[...]).astype(BF16)

    ang = pos_ref[...].astype(F32) * invf_ref[...]
    cosf = jnp.cos(ang) * cmask_ref[...]
    sinf = jnp.sin(ang) * ssign_ref[...]

    o = SSM_WIDTH
    cq = _rms(proj[:, o:o + MLA_Q_LORA], gq_ref[...]).astype(BF16)
    o += MLA_Q_LORA
    ckv = _rms(proj[:, o:o + MLA_KV_LORA], gkv_ref[...]).astype(BF16)
    o += MLA_KV_LORA
    kr = proj[:, o:o + HEAD_PAD]
    krs = proj[:, o + HEAD_PAD:o + 2 * HEAD_PAD]
    kr_rot = kr * cosf + krs * sinf

    qq = _bdot(cq, wq_ref[...])
    kv = _bdot(ckv, wkv_ref[...])
    scale = (MLA_NOPE + MLA_ROPE) ** -0.5 * math.log2(math.e)
    hw = MLA_HEADS * HEAD_PAD
    for h in range(MLA_HEADS):
        sl = slice(h * HEAD_PAD, (h + 1) * HEAD_PAD)
        sl2 = slice(hw + h * HEAD_PAD, hw + (h + 1) * HEAD_PAD)
        q_ref[:, sl] = ((qq[:, sl] * cosf + qq[:, sl2] * sinf) * scale).astype(BF16)
        k_ref[:, sl] = (kv[:, sl] + kr_rot).astype(BF16)
    v_ref[...] = (kv[:, hw:] + vone_ref[...]).astype(BF16)


def _in_proj(x2, pos2, gmix, wa, wg, bg, gq, wq2, gkv, wkv2, invf, cmask, ssign, vone, B, L, tl):
    N = B * L
    nl = L // tl
    hw = MLA_HEADS * HEAD_PAD
    row = lambda b, i: (b * nl + i, 0)
    consts = [gmix, wa, wg, bg, gq, wq2, gkv, wkv2, invf, cmask, ssign, vone]
    return pl.pallas_call(
        _in_proj_kernel,
        grid=(B, nl),
        in_specs=[pl.BlockSpec((tl, D_MODEL), row), pl.BlockSpec((tl, 1), row)]
        + [_full(c.shape) for c in consts],
        out_specs=[pl.BlockSpec((tl, SSM_WIDTH), lambda b, i: (i, b)),
                   pl.BlockSpec((tl, hw), row), pl.BlockSpec((tl, hw), row),
                   pl.BlockSpec((tl, hw), row), pl.BlockSpec((tl, 2 * D_MODEL), row)],
        out_shape=[jax.ShapeDtypeStruct((L, B * SSM_WIDTH), F32),
                   jax.ShapeDtypeStruct((N, hw), BF16), jax.ShapeDtypeStruct((N, hw), BF16),
                   jax.ShapeDtypeStruct((N, hw), BF16), jax.ShapeDtypeStruct((N, 2 * D_MODEL), BF16)],
        compiler_params=_cparams(("arbitrary", "arbitrary")),
        name="in_proj",
    )(x2, pos2, *consts)


def _s5_disc_kernel(lr_ref, li_ref, ldt_ref, bre_ref, bim_ref, ar_ref, ai_ref, bbr_ref, bbi_ref):
    lr = lr_ref[...]
    li = li_ref[...]
    dt = jnp.exp(ldt_ref[...])
    mag = jnp.exp(lr * dt)
    ar = mag * jnp.cos(li * dt)
    ai = mag * jnp.sin(li * dt)
    nr = ar - 1.0
    den = lr * lr + li * li
    fr = (nr * lr + ai * li) / den
    fi = (ai * lr - nr * li) / den
    ar_ref[...] = ar
    ai_ref[...] = ai
    bre = bre_ref[...]
    bim = bim_ref[...]
    bbr_ref[...] = fr * bre - fi * bim
    bbi_ref[...] = fr * bim + fi * bre


def _s5_discretise(lam_re, lam_im, log_dt, b_re, b_im):
    n = lam_re.size
    ldt = jnp.broadcast_to(log_dt[..., None], lam_re.shape).reshape(n, 1)
    sds = jax.ShapeDtypeStruct
    ar, ai, bbr, bbi = pl.pallas_call(
        _s5_disc_kernel,
        out_shape=[sds((n, 1), F32), sds((n, 1), F32), sds((n, SSM_GROUP), F32), sds((n, SSM_GROUP), F32)],
        name="s5_discretise",
    )(lam_re.reshape(n, 1), lam_im.reshape(n, 1), ldt,
      b_re.reshape(n, SSM_GROUP), b_im.reshape(n, SSM_GROUP))
    return (ar.reshape(lam_re.shape), ai.reshape(lam_re.shape),
            bbr.reshape(b_re.shape), bbi.reshape(b_re.shape))


def _s5_kernel(uf_ref, ub_ref, bm_ref, cm_ref, ar_ref, ai_ref, yf_ref, yb_ref,
               st_ref, sf_ref, sb_ref, *, tt, nb):
    S = SSM_STATES
    CS = SSM_CHUNK_STATES

    @pl.when(pl.program_id(0) == 0)
    def _():
        st_ref[...] = jnp.zeros_like(st_ref)

    for d, (u_ref, s_ref) in enumerate(((uf_ref, sf_ref), (ub_ref, sb_ref))):
        ub = u_ref[...].astype(BF16)
        for j in range(SSM_CHUNKS):
            bu = _bdot(ub[:, j * SSM_CHUNK:(j + 1) * SSM_CHUNK], bm_ref[d, j])
            s_ref[:, j * CS:(j + 1) * CS] = bu[:, :CS]
            s_ref[:, S + j * CS:S + (j + 1) * CS] = bu[:, CS:]

    for c in range(S // CS):
        lre = slice(c * CS, (c + 1) * CS)
        lim = slice(S + c * CS, S + (c + 1) * CS)
        arf = jnp.broadcast_to(ar_ref[0, :, lre], (nb, CS))
        aif = jnp.broadcast_to(ai_ref[0, :, lre], (nb, CS))
        arb = jnp.broadcast_to(ar_ref[1, :, lre], (nb, CS))
        aib = jnp.broadcast_to(ai_ref[1, :, lre], (nb, CS))

        def body(t, carry):
            fr, fi, br, bi = carry
            rf = pl.multiple_of(t * nb, nb)
            rb = pl.multiple_of((tt - 1 - t) * nb, nb)
            nfr = arf * fr - aif * fi + sf_ref[pl.ds(rf, nb), lre]
            nfi = arf * fi + aif * fr + sf_ref[pl.ds(rf, nb), lim]
            sf_ref[pl.ds(rf, nb), lre] = nfr
            sf_ref[pl.ds(rf, nb), lim] = nfi
            nbr = arb * br - aib * bi + sb_ref[pl.ds(rb, nb), lre]
            nbi = arb * bi + aib * br + sb_ref[pl.ds(rb, nb), lim]
            sb_ref[pl.ds(rb, nb), lre] = nbr
            sb_ref[pl.ds(rb, nb), lim] = nbi
            return nfr, nfi, nbr, nbi

        init = (st_ref[0, :, lre], st_ref[0, :, lim], st_ref[1, :, lre], st_ref[1, :, lim])
        fr, fi, br, bi = lax.fori_loop(0, tt, body, init)
        st_ref[0, :, lre] = fr
        st_ref[0, :, lim] = fi
        st_ref[1, :, lre] = br
        st_ref[1, :, lim] = bi

    for d, (s_ref, y_ref) in enumerate(((sf_ref, yf_ref), (sb_ref, yb_ref))):
        for j in range(SSM_CHUNKS):
            sre = s_ref[:, j * CS:(j + 1) * CS].astype(BF16)
            sim = s_ref[:, S + j * CS:S + (j + 1) * CS].astype(BF16)
            y_ref[:, j * SSM_CHUNK:(j + 1) * SSM_CHUNK] = (
                _bdot(sre, cm_ref[d, j, :CS, :]) + _bdot(sim, cm_ref[d, j, CS:, :]))


def _s5(u_tm, bm, cm, ar, ai, B, L, tt):
    nblk = L // tt
    R = tt * B
    blk = lambda idx: pl.BlockSpec((R, SSM_WIDTH), idx)
    fwd = lambda i: (i, 0)
    bwd = lambda i: (nblk - 1 - i, 0)
    return pl.pallas_call(
        functools.partial(_s5_kernel, tt=tt, nb=B),
        grid=(nblk,),
        in_specs=[blk(fwd), blk(bwd), _full(bm.shape), _full(cm.shape), _full(ar.shape), _full(ai.shape)],
        out_specs=[blk(fwd), blk(bwd)],
        out_shape=[jax.ShapeDtypeStruct((L * B, SSM_WIDTH), F32)] * 2,
        scratch_shapes=[pltpu.VMEM((2, B, 2 * SSM_STATES), F32),
                        pltpu.VMEM((R, 2 * SSM_STATES), F32),
                        pltpu.VMEM((R, 2 * SSM_STATES), F32)],
        compiler_params=_cparams(("arbitrary",)),
        name="s5_scan",
    )(u_tm, u_tm, bm, cm, ar, ai)


def _attn_kernel(q_ref, k_ref, v_ref, o_ref, *, ck):
    q = q_ref[...]
    tq = q.shape[0]
    m = jnp.full((tq, 1), _NEG_INF, F32)
    acc = jnp.zeros((tq, HEAD_PAD), F32)
    for j in range(k_ref.shape[0] // ck):
        s = lax.dot_general(q, k_ref[j * ck:(j + 1) * ck, :], (((1,), (1,)), ((), ())),
                            preferred_element_type=F32)
        m_new = jnp.maximum(m, jnp.max(s, axis=-1, keepdims=True))
        p = jnp.exp2(s - m_new).astype(BF16)
        acc = acc * jnp.exp2(m - m_new) + _bdot(p, v_ref[j * ck:(j + 1) * ck, :])
        m = m_new
    o_ref[...] = (acc / acc[:, MLA_V:MLA_V + 1]).astype(BF16)


def _attention(q, k, v, B, L, tq, ck):
    nq = L // tq
    return pl.pallas_call(
        functools.partial(_attn_kernel, ck=ck),
        grid=(B, MLA_HEADS, nq),
        in_specs=[pl.BlockSpec((tq, HEAD_PAD), lambda b, h, i: (b * nq + i, h)),
                  pl.BlockSpec((L, HEAD_PAD), lambda b, h, i: (b, h)),
                  pl.BlockSpec((L, HEAD_PAD), lambda b, h, i: (b, h))],
        out_specs=pl.BlockSpec((tq, HEAD_PAD), lambda b, h, i: (b * nq + i, h)),
        out_shape=jax.ShapeDtypeStruct(q.shape, BF16),
        compiler_params=_cparams(("arbitrary", "arbitrary", "arbitrary")),
        name="mla_attention",
    )(q, k, v)


def _merge_kernel(x_ref, yf_ref, yb_ref, u_ref, at_ref, gate_ref, d_ref, wglu_ref, wmo_ref, wout_ref,
                  h_ref):
    y = yf_ref[...] + yb_ref[...] + d_ref[...] * u_ref[...]
    y = _gelu(y).astype(BF16)
    glu = _bdot(y, wglu_ref[...])
    branch_ssm = glu[:, :D_MODEL] * jax.nn.sigmoid(glu[:, D_MODEL:])
    branch_mla = _bdot(at_ref[...], wmo_ref[...])
    g = gate_ref[...].astype(F32)
    merged = (g[:, :D_MODEL] * branch_ssm + g[:, D_MODEL:] * branch_mla).astype(BF16)
    h_ref[...] = x_ref[...] + _bdot(merged, wout_ref[...])


def _merge(x2, yf_tm, yb_tm, u_tm, attn, gates, d_skip, wglu, wmo, wout, B, L, tl):
    nl = L // tl
    row = lambda b, i: (b * nl + i, 0)
    tm = pl.BlockSpec((tl, SSM_WIDTH), lambda b, i: (i, b))
    consts = [d_skip, wglu, wmo, wout]
    return pl.pallas_call(
        _merge_kernel,
        grid=(B, nl),
        in_specs=[pl.BlockSpec((tl, D_MODEL), row), tm, tm, tm,
                  pl.BlockSpec((tl, attn.shape[1]), row), pl.BlockSpec((tl, 2 * D_MODEL), row)]
        + [_full(c.shape) for c in consts],
        out_specs=pl.BlockSpec((tl, D_MODEL), row),
        out_shape=jax.ShapeDtypeStruct(x2.shape, F32),
        compiler_params=_cparams(("arbitrary", "arbitrary")),
        name="merge_out",
    )(x2, yf_tm, yb_tm, u_tm, attn, gates, *consts)


def _mem_kv_kernel(mem_ref, g_ref, w_ref, kv_ref):
    mn = _rms(mem_ref[...], g_ref[...]).astype(BF16)
    kv_ref[...] = _bdot(mn, w_ref[...]).astype(BF16)


def _mem_kv(mem2, g, w, tm):
    n = mem2.shape[0]
    return pl.pallas_call(
        _mem_kv_kernel,
        grid=(n // tm,),
        in_specs=[pl.BlockSpec((tm, D_MODEL), lambda i: (i, 0)), _full(g.shape), _full(w.shape)],
        out_specs=pl.BlockSpec((tm, 2 * D_MODEL), lambda i: (i, 0)),
        out_shape=jax.ShapeDtypeStruct((n, 2 * D_MODEL), BF16),
        compiler_params=_cparams(("arbitrary",)),
        name="mem_kv",
    )(mem2, g, w)


def _mem_attn_kernel(h_ref, kv_ref, g_ref, wq_ref, wo_ref, o_ref):
    h = h_ref[...]
    q = _bdot(_rms(h, g_ref[...]).astype(BF16), wq_ref[...])
    scale = MEM_HEAD_DIM ** -0.5
    outs = []
    for hd in range(MEM_HEADS):
        sl = slice(hd * MEM_HEAD_DIM, (hd + 1) * MEM_HEAD_DIM)
        kh = kv_ref[:, sl]
        vh = kv_ref[:, D_MODEL + hd * MEM_HEAD_DIM:D_MODEL + (hd + 1) * MEM_HEAD_DIM]
        s = lax.dot_general(q[:, sl].astype(BF16), kh, (((1,), (1,)), ((), ())),
                            preferred_element_type=F32) * scale
        p = jnp.exp(s - jnp.max(s, axis=-1, keepdims=True))
        p = p / jnp.sum(p, axis=-1, keepdims=True)
        outs.append(_bdot(p.astype(BF16), vh).astype(BF16))
    o = jnp.concatenate(outs, axis=-1)
    o_ref[...] = h + _bdot(o, wo_ref[...])


def _mem_attn(h2, kv, g, wq, wo, B, L, M, tl):
    nl = L // tl
    row = lambda b, i: (b * nl + i, 0)
    consts = [g, wq, wo]
    return pl.pallas_call(
        _mem_attn_kernel,
        grid=(B, nl),
        in_specs=[pl.BlockSpec((tl, D_MODEL), row), pl.BlockSpec((M, 2 * D_MODEL), lambda b, i: (b, 0))]
        + [_full(c.shape) for c in consts],
        out_specs=pl.BlockSpec((tl, D_MODEL), row),
        out_shape=jax.ShapeDtypeStruct(h2.shape, F32),
        compiler_params=_cparams(("arbitrary", "arbitrary")),
        name="mem_attention",
    )(h2, kv, *consts)


_NEG_INF = float("-inf")
_BIG = 3.0e38
PEER_EXPERTS = PEER_N_KEYS * PEER_N_KEYS


def _topk_rows(problems, k):
    ss = [s for s, _ in problems]
    vals = [[] for _ in problems]
    pays = [[] for _ in problems]
    for _ in range(k):
        for i, (_, payload) in enumerate(problems):
            m = jnp.max(ss[i], axis=0, keepdims=True)
            cand = jnp.where(ss[i] == m, payload, _BIG)
            p = jnp.min(cand, axis=0, keepdims=True)
            vals[i].append(m)
            pays[i].append(p)
            ss[i] = jnp.where(cand == p, _NEG_INF, ss[i])
    return [(jnp.concatenate(v, axis=0), jnp.concatenate(p, axis=0)) for v, p in zip(vals, pays)]


def _peer_candidates(s1, i1, s2, i2, T):
    K = PEER_TOPK
    E = float(PEER_EXPERTS)
    r8 = lax.broadcasted_iota(jnp.int32, (8, T), 0)
    r8f = r8.astype(F32)
    r16f = lax.broadcasted_iota(jnp.int32, (K, T), 0).astype(F32)
    cs = [s1[0:1] + s2]
    pay = [r16f * E + (i1[0:1] * PEER_N_KEYS + i2)]
    for r1 in range(1, 8):
        ok = r8 < K // (r1 + 1)
        cs.append(jnp.where(ok, s1[r1:r1 + 1] + s2[0:8], _NEG_INF))
        pay.append((r8f + r1 * K) * E + (i1[r1:r1 + 1] * PEER_N_KEYS + i2[0:8]))
    cs.append(s1[8:K] + s2[0:1])
    pay.append((r8f + 8.0) * (K * E) + (i1[8:K] * PEER_N_KEYS + i2[0:1]))
    return jnp.concatenate(cs, axis=0), jnp.concatenate(pay, axis=0)


def _peer_topk_kernel(h_ref, g_ref, wq_ref, keys_ref, hn_ref, idx_ref, gate_ref, q_scr, *, T, hp):
    hd = pl.program_id(1)

    @pl.when(hd == 0)
    def _():
        hn = _rms(h_ref[...], g_ref[...])
        hn_ref[...] = hn
        q = _bdot(hn.astype(BF16), wq_ref[...]).astype(BF16)
        for j in range(2 * PEER_HEADS):
            q_scr[j] = q[:, j * PEER_HALF:(j + 1) * PEER_HALF]

    n_iota = lax.broadcasted_iota(jnp.int32, (PEER_N_KEYS, T), 0).astype(F32)
    K = PEER_TOPK
    problems = []
    for j in range(2 * hp):
        hs = 2 * hp * hd + j
        s = lax.dot_general(keys_ref[hs], q_scr[hs], (((1,), (1,)), ((), ())),
                            preferred_element_type=F32)
        problems.append((s, n_iota))
    tops = _topk_rows(problems, K)
    cands = [_peer_candidates(*tops[2 * i], *tops[2 * i + 1], T) for i in range(hp)]
    for i, (top_s, top_pay) in enumerate(_topk_rows(cands, K)):
        expert = top_pay - jnp.floor(top_pay * (1.0 / PEER_EXPERTS)) * PEER_EXPERTS
        e = jnp.exp(top_s - top_s[0:1])
        gate_ref[i * K:(i + 1) * K, :] = e / jnp.sum(e, axis=0, keepdims=True)
        idx_ref[i * K:(i + 1) * K, :] = expert.astype(jnp.int32) * TABLE_ROWS


def _peer_topk(h2, g, wq, keys, T, hp):
    N = h2.shape[0]
    K = PEER_TOPK * hp
    return pl.pallas_call(
        functools.partial(_peer_topk_kernel, T=T, hp=hp),
        grid=(N // T, PEER_HEADS // hp),
        in_specs=[pl.BlockSpec((T, D_MODEL), lambda i, h: (i, 0)), _full(g.shape), _full(wq.shape),
                  _full(keys.shape)],
        out_specs=[pl.BlockSpec((T, D_MODEL), lambda i, h: (i, 0)),
                   pl.BlockSpec((K, T), lambda i, h: (h, i)),
                   pl.BlockSpec((K, T), lambda i, h: (h, i))],
        out_shape=[jax.ShapeDtypeStruct((N, D_MODEL), F32),
                   jax.ShapeDtypeStruct((PEER_SLOTS, N), jnp.int32),
                   jax.ShapeDtypeStruct((PEER_SLOTS, N), F32)],
        scratch_shapes=[pltpu.VMEM((2 * PEER_HEADS, T, PEER_HALF), BF16)],
        compiler_params=_cparams(("arbitrary", "arbitrary")),
        name="peer_topk",
    )(h2, g, wq, keys)


PEER_SLOTS = PEER_HEADS * PEER_TOPK
TABLE_ROWS = D_MODEL // (2 * LANES)
PEER_VMEM_LIMIT = 56 * 1024 * 1024


GATHER_ROWS = PEER_SLOTS * TABLE_ROWS
GATHER_BF16_ROWS = 2 * GATHER_ROWS


def _chunk_mask():
    shape = (SUBLANES, GATHER_BF16_ROWS)
    c = lax.broadcasted_iota(jnp.int32, shape, 0)
    j = lax.broadcasted_iota(jnp.int32, shape, 1) & (2 * TABLE_ROWS - 1)
    return c == (j >> 1) + TABLE_ROWS * (j & 1)


def _split_bf16(x):
    hi = x.astype(BF16)
    return hi, (x - hi.astype(F32)).astype(BF16)


GROUP = SUBLANES
PEER_TILE = 128


def _gather_group(idx_ref, tab_ref, g_buf, t0):
    for j in range(GROUP):
        for k in range(PEER_SLOTS):
            row = pl.multiple_of(idx_ref[t0 + j, k], TABLE_ROWS)
            g_buf[j, k * TABLE_ROWS:(k + 1) * TABLE_ROWS, :] = tab_ref[pl.ds(row, TABLE_ROWS), :]


def _pipelined_groups(idx_ref, tab_ref, g_a, g_b, tb, consume):
    last = tb - GROUP
    _gather_group(idx_ref, tab_ref, g_a, 0)

    def pair(p, carry):
        ta = pl.multiple_of(p * 2 * GROUP, GROUP)
        tb_ = ta + GROUP
        _gather_group(idx_ref, tab_ref, g_b, tb_)
        consume(g_a, ta)
        _gather_group(idx_ref, tab_ref, g_a, jnp.minimum(ta + 2 * GROUP, last))
        consume(g_b, tb_)
        return carry

    lax.fori_loop(0, tb // (2 * GROUP), pair, 0)


def _row_to_chunks(row):
    return jnp.concatenate([row[:, c * LANES:(c + 1) * LANES] for c in range(SUBLANES)], axis=0)


def _peer_u_kernel(idx_ref, x_ref, gate_ref, tab_ref, e8_ref, w_ref, g_a, g_b, *, tb):
    mask = _chunk_mask()
    rows = lax.broadcasted_iota(jnp.int32, (SUBLANES, GATHER_BF16_ROWS), 0)
    nt = (((1,), (1,)), ((), ()))

    def consume(g_buf, t0):
        dall = jnp.zeros((SUBLANES, GATHER_BF16_ROWS), F32)
        for j in range(GROUP):
            gb = pltpu.bitcast(g_buf[j], BF16)
            x8 = _row_to_chunks(x_ref[pl.ds(t0 + j, 1), :]).astype(BF16)
            r = lax.dot_general(x8, gb, nt, preferred_element_type=F32)
            d = jnp.sum(jnp.where(mask, r, 0.0), axis=0, keepdims=True)
            dall = jnp.where(rows == j, d, dall)
        hi, lo = _split_bf16(dall)
        act = _bdot(hi, e8_ref[...]) + _bdot(lo, e8_ref[...])
        w_ref[pl.ds(t0, GROUP), :] = gate_ref[pl.ds(t0, GROUP), :] * _gelu(act)

    _pipelined_groups(idx_ref, tab_ref, g_a, g_b, tb, consume)


def _peer_v_kernel(idx_ref, w_ref, h_ref, g_ref, tab_ref, e8t_ref, o_ref, g_a, g_b, *, tb):
    mask = _chunk_mask()
    shape = (SUBLANES, GATHER_BF16_ROWS)

    def group(g, carry):
        t0 = pl.multiple_of(g * GROUP, GROUP)
        wb = w_ref[pl.ds(t0, GROUP), :].astype(BF16)
        rep = _bdot(wb, e8t_ref[...])
        outs = []
        for j in range(GROUP):
            rows = [tab_ref[pl.ds(pl.multiple_of(idx_ref[t0 + j, k], TABLE_ROWS), TABLE_ROWS), :]
                    for k in range(PEER_SLOTS)]
            gb = pltpu.bitcast(jnp.concatenate(rows, axis=0), BF16)
            wexp = jnp.where(mask, jnp.broadcast_to(rep[j:j + 1], shape), 0.0).astype(BF16)
            outs.append(_bdot(wexp, gb))
        peer = jnp.concatenate(
            [jnp.concatenate([outs[j][c:c + 1, :] for j in range(GROUP)], axis=0) for c in range(SUBLANES)],
            axis=1)
        o_ref[pl.ds(t0, GROUP), :] = _rms(h_ref[pl.ds(t0, GROUP), :] + peer, g_ref[...])
        return carry

    lax.fori_loop(0, tb // GROUP, group, 0)


def _smem_tile(tb):
    return pl.BlockSpec((tb, PEER_SLOTS), lambda i: (i, 0), memory_space=pltpu.SMEM)


def _peer_cparams():
    return pltpu.CompilerParams(dimension_semantics=("arbitrary",), vmem_limit_bytes=PEER_VMEM_LIMIT)


def _gather_scratch():
    return [pltpu.VMEM((GROUP, GATHER_ROWS, LANES), jnp.int32)] * 2


def _slot_expand():
    return jnp.repeat(jnp.eye(PEER_SLOTS, dtype=BF16), 2 * TABLE_ROWS, axis=0)


def _peer_u(idx4, hn, gates, tab):
    N = idx4.shape[0]
    tb = min(PEER_TILE, N)
    e8 = _slot_expand()
    tile = pl.BlockSpec((tb, PEER_SLOTS), lambda i: (i, 0))
    return pl.pallas_call(
        functools.partial(_peer_u_kernel, tb=tb),
        grid=(N // tb,),
        in_specs=[_smem_tile(tb), pl.BlockSpec((tb, D_MODEL), lambda i: (i, 0)), tile,
                  _full(tab.shape), _full(e8.shape)],
        out_specs=tile,
        out_shape=jax.ShapeDtypeStruct((N, PEER_SLOTS), F32),
        scratch_shapes=_gather_scratch(),
        compiler_params=_peer_cparams(),
        name="peer_u_gather",
    )(idx4, hn, gates, tab, e8)


def _peer_v_final(idx4, w, h2, g_final, tab):
    N = idx4.shape[0]
    tb = min(PEER_TILE, N)
    e8t = _slot_expand().T
    g8 = g_final.reshape(1, D_MODEL)
    row = pl.BlockSpec((tb, D_MODEL), lambda i: (i, 0))
    return pl.pallas_call(
        functools.partial(_peer_v_kernel, tb=tb),
        grid=(N // tb,),
        in_specs=[_smem_tile(tb), pl.BlockSpec((tb, PEER_SLOTS), lambda i: (i, 0)), row, _full(g8.shape),
                  _full(tab.shape), _full(e8t.shape)],
        out_specs=row,
        out_shape=jax.ShapeDtypeStruct((N, D_MODEL), F32),
        scratch_shapes=_gather_scratch(),
        compiler_params=_peer_cparams(),
        name="peer_v_gather_final",
    )(idx4, w, h2, g8, tab, e8t)


def _pack_table(t):
    E = t.shape[0]
    b = lax.bitcast_convert_type(t.astype(BF16), jnp.uint16).astype(jnp.uint32)
    b = b.reshape(E, 2, TABLE_ROWS, LANES)
    word = b[:, 0] | (b[:, 1] << 16)
    return lax.bitcast_convert_type(word, jnp.int32).reshape(E * TABLE_ROWS, LANES)


def _peer_and_final_norm(h2, g_peer, w_peer_q, sub_keys, u_emb, v_emb, g_final):
    N = h2.shape[0]
    keys = sub_keys.reshape(2 * PEER_HEADS, PEER_N_KEYS, PEER_HALF).astype(BF16)
    hn, idx_t, gate_t = _peer_topk(h2, g_peer[None], w_peer_q.astype(BF16), keys, min(256, N), 4)
    idx4 = idx_t.T
    w = _peer_u(idx4, hn, gate_t.T, _pack_table(u_emb))
    return _peer_v_final(idx4, w, h2, g_final, _pack_table(v_emb))


def _rope_swap(w):
    half = MLA_ROPE // 2
    return jnp.concatenate([w[..., half:], w[..., :half]], axis=-1)


def _pad_cols(w, width):
    return jnp.pad(w, ((0, 0), (0, width - w.shape[1])))


def _prep_mixer_weights(w_in, w_uq, w_ukv, w_mla_o):
    o0, o1, o2, o3 = SSM_WIDTH, SSM_WIDTH + MLA_Q_LORA, SSM_WIDTH + MLA_Q_LORA + MLA_KV_LORA, None
    o3 = o2 + MLA_ROPE
    w_kr = w_in[:, o2:o3]
    zl = jnp.zeros((D_MODEL, MLA_NOPE), F32)
    kr_pad = _pad_cols(jnp.concatenate([zl, w_kr], axis=1), HEAD_PAD)
    krs_pad = _pad_cols(jnp.concatenate([zl, _rope_swap(w_kr)], axis=1), HEAD_PAD)
    wa = jnp.concatenate([w_in[:, :o2], kr_pad, krs_pad], axis=1).astype(BF16)
    wg = w_in[:, o3:].astype(BF16)

    wq = w_uq.reshape(MLA_Q_LORA, MLA_HEADS, MLA_NOPE + MLA_ROPE)
    pad = HEAD_PAD - MLA_NOPE - MLA_ROPE
    wq_pad = jnp.pad(wq, ((0, 0), (0, 0), (0, pad)))
    wq_sw = jnp.concatenate([jnp.zeros_like(wq[..., :MLA_NOPE]), _rope_swap(wq[..., MLA_NOPE:])], axis=-1)
    wq_sw = jnp.pad(wq_sw, ((0, 0), (0, 0), (0, pad)))
    hw = MLA_HEADS * HEAD_PAD
    wq2 = jnp.concatenate([wq_pad.reshape(MLA_Q_LORA, hw), wq_sw.reshape(MLA_Q_LORA, hw)], axis=1).astype(BF16)

    wkv = w_ukv.reshape(MLA_KV_LORA, MLA_HEADS, MLA_NOPE + MLA_V)
    wk = jnp.pad(wkv[..., :MLA_NOPE], ((0, 0), (0, 0), (0, HEAD_PAD - MLA_NOPE))).reshape(MLA_KV_LORA, hw)
    wv = jnp.pad(wkv[..., MLA_NOPE:], ((0, 0), (0, 0), (0, HEAD_PAD - MLA_V))).reshape(MLA_KV_LORA, hw)
    wkv2 = jnp.concatenate([wk, wv], axis=1).astype(BF16)

    wmo = w_mla_o.reshape(MLA_HEADS, MLA_V, D_MODEL)
    wmo = jnp.pad(wmo, ((0, 0), (0, HEAD_PAD - MLA_V), (0, 0))).reshape(hw, D_MODEL).astype(BF16)
    return wa, wg, wq2, wkv2, wmo


def _rope_lane_consts():
    lane = jnp.arange(HEAD_PAD)
    half = MLA_ROPE // 2
    inv_freq = 1.0 / (ROPE_THETA ** (jnp.arange(0, MLA_ROPE, 2, dtype=F32) / MLA_ROPE))
    r = lane - MLA_NOPE
    in_rope = (r >= 0) & (r < MLA_ROPE)
    invf = jnp.where(in_rope, inv_freq[jnp.clip(r, 0, MLA_ROPE - 1) % half], 0.0)
    cmask = (lane < MLA_NOPE + MLA_ROPE).astype(F32)
    ssign = jnp.where(in_rope, jnp.where(r < half, -1.0, 1.0), 0.0)
    vone = (jnp.arange(MLA_HEADS * HEAD_PAD) % HEAD_PAD == MLA_V).astype(F32)
    return invf[None].astype(F32), cmask[None], ssign[None].astype(F32), vone[None]


def _s5_block_weights(bb_re, bb_im, c_re, c_im):
    gpc = SSM_CHUNK // SSM_GROUP
    eye = jnp.eye(gpc, dtype=F32)

    def bblock(bb):
        bb = bb.reshape(2, SSM_CHUNKS, gpc, SSM_STATE, SSM_GROUP)
        return jnp.einsum('ab,djapc->djacbp', eye, bb).reshape(2, SSM_CHUNKS, SSM_CHUNK, SSM_CHUNK_STATES)

    def cblock(c):
        c = c.reshape(2, SSM_CHUNKS, gpc, SSM_GROUP, SSM_STATE)
        return jnp.einsum('ab,djacp->djbpac', eye, c).reshape(2, SSM_CHUNKS, SSM_CHUNK_STATES, SSM_CHUNK)

    bm = jnp.concatenate([bblock(bb_re), bblock(bb_im)], axis=-1).astype(BF16)
    cm = jnp.concatenate([cblock(c_re), -cblock(c_im)], axis=-2).astype(BF16)
    return bm, cm


def _front(l, h, mem, positions, g_mix, w_in, b_gate, ssm_lam_re, ssm_lam_im, ssm_log_dt, ssm_b_re,
           ssm_b_im, ssm_c_re, ssm_c_im, ssm_d, w_ssm_glu, g_q_lora, w_uq, g_kv_lora, w_ukv, w_mla_o,
           w_out, g_mem_q, g_mem_kv, w_mem_q, w_mem_kv, w_mem_o, B, L):
    M = mem.shape[1]
    N = B * L
    tl = min(512, L)
    pos2 = positions.reshape(N, 1)
    invf, cmask, ssign, vone = _rope_lane_consts()
    wa, wg, wq2, wkv2, wmo = _prep_mixer_weights(w_in[l], w_uq[l], w_ukv[l], w_mla_o[l])
    u_tm, q, k, v, gates = _in_proj(
        h, pos2, g_mix[l][None], wa, wg, b_gate[l][None], g_q_lora[l][None], wq2,
        g_kv_lora[l][None], wkv2, invf, cmask, ssign, vone, B, L, tl)
    ar, ai, bbr, bbi = _s5_discretise(ssm_lam_re[l], ssm_lam_im[l], ssm_log_dt[l],
                                      ssm_b_re[l], ssm_b_im[l])
    bm, cm = _s5_block_weights(bbr, bbi, ssm_c_re[l], ssm_c_im[l])
    u_rows = u_tm.reshape(L * B, SSM_WIDTH)
    yf, yb = _s5(u_rows, bm, cm, ar.reshape(2, 1, SSM_STATES), ai.reshape(2, 1, SSM_STATES),
                 B, L, min(32, L))
    attn = _attention(q, k, v, B, L, min(512, L), min(1024, L))
    h = _merge(h, yf.reshape(L, B * SSM_WIDTH), yb.reshape(L, B * SSM_WIDTH), u_tm, attn, gates,
               ssm_d[l][None], w_ssm_glu[l].astype(BF16), wmo, w_out[l].astype(BF16), B, L, tl)
    kv_mem = _mem_kv(mem.reshape(B * M, D_MODEL), g_mem_kv[l][None], w_mem_kv[l].astype(BF16), M)
    return _mem_attn(h, kv_mem, g_mem_q[l][None], w_mem_q[l].astype(BF16), w_mem_o[l].astype(BF16),
                     B, L, M, tl)


def kernel(x, mem, positions, g_mix, w_in, b_gate, ssm_lam_re, ssm_lam_im, ssm_log_dt, ssm_b_re, ssm_b_im, ssm_c_re, ssm_c_im, ssm_d, w_ssm_glu, g_q_lora, w_uq, g_kv_lora, w_ukv, w_mla_o, w_out, g_mem_q, g_mem_kv, w_mem_q, w_mem_kv, w_mem_o, g_peer, w_peer_q, peer_sub_keys, peer_u, peer_v, g_final):
    B, L, _ = x.shape
    assert g_mix.shape[0] == 1, "the final norm is fused into the last (only) layer's PEER kernel"
    h = _front(0, x.reshape(B * L, D_MODEL), mem, positions, g_mix, w_in, b_gate, ssm_lam_re, ssm_lam_im,
               ssm_log_dt, ssm_b_re, ssm_b_im, ssm_c_re, ssm_c_im, ssm_d, w_ssm_glu, g_q_lora, w_uq,
               g_kv_lora, w_ukv, w_mla_o, w_out, g_mem_q, g_mem_kv, w_mem_q, w_mem_kv, w_mem_o, B, L)
    out = _peer_and_final_norm(h, g_peer[0], w_peer_q[0], peer_sub_keys[0], peer_u[0], peer_v[0], g_final)
    return out.reshape(B, L, D_MODEL)
```
